```python
import math
import jax, jax.numpy as jnp
from jax import lax
import numpy as np

D_MODEL = 1024
BATCH = 4
SEQ = 8192
DEPTH = 4
DEC_BATCH = 8
DEC_SEQ = 4096
PAST_LEN = 128

ROPE_THETA = 500000.0
LN_EPS = 1e-5
ALPHA = (2.0 * DEPTH) ** 0.25
BETA = (8.0 * DEPTH) ** -0.25

A_HD = 64
A_HEADS = D_MODEL // (2 * A_HD)
A_QK = 2 * A_HEADS * A_HD
A_V = A_HEADS * 2 * A_HD
A_QBLK = 128

B_HD = 64
B_HEADS = D_MODEL // B_HD
B_GROUPS = ((128, 1), (512, 4), (2048, 16))
B_W = B_HEADS * B_HD
B_BLK = 64

kernel_name = "hybrid_diffattn_dilated_deepnorm_encoder"


def layer_norm(x, g, b):
    xf = x.astype(jnp.float32)
    mu = jnp.mean(xf, axis=-1, keepdims=True)
    var = jnp.mean(jnp.square(xf - mu), axis=-1, keepdims=True)
    y = (xf - mu) * lax.rsqrt(var + LN_EPS) * g.astype(jnp.float32) + b.astype(jnp.float32)
    return y.astype(x.dtype)


def rope_partial(x, pos):
    hd = x.shape[-1]
    rd = hd // 4
    half = rd // 2
    inv = ROPE_THETA ** (-jnp.arange(half, dtype=jnp.float32) / half)
    ang = pos.astype(jnp.float32)[:, None] * inv[None, :]
    cos = jnp.cos(ang)[:, None, :]
    sin = jnp.sin(ang)[:, None, :]
    xf = x.astype(jnp.float32)
    x1, x2, rest = xf[..., :half], xf[..., half:rd], xf[..., rd:]
    out = jnp.concatenate([x1 * cos - x2 * sin, x2 * cos + x1 * sin, rest], axis=-1)
    return out.astype(x.dtype)


def diff_attention(q, k, v, lam):
    B, S, H, _, hd = q.shape
    nq = S // A_QBLK
    scale = 1.0 / math.sqrt(hd)
    qb = q.reshape(B, nq, A_QBLK, H, 2, hd).transpose(1, 0, 2, 3, 4, 5)

    def one_block(qblk):
        s = jnp.einsum('bqhcd,bkhcd->bhcqk', qblk, k).astype(jnp.float32) * scale
        p = jax.nn.softmax(s, axis=-1)
        a = p[:, :, 0] - lam * p[:, :, 1]
        return jnp.einsum('bhqk,bkhe->bqhe', a.astype(v.dtype), v)

    o = lax.map(one_block, qb)
    return o.transpose(1, 0, 2, 3, 4).reshape(B, S, H, v.shape[-1])


def mixer_a(x, w_in, lq1, lk1, lq2, lk2, subln_g, w_out, layer_idx):
    B, S, _ = x.shape
    h = x @ w_in
    q, k, v, g = jnp.split(h, [A_QK, 2 * A_QK, 2 * A_QK + A_V], axis=-1)
    pos = jnp.arange(S)
    q = rope_partial(q.reshape(B, S, 2 * A_HEADS, A_HD), pos).reshape(B, S, A_HEADS, 2, A_HD)
    k = rope_partial(k.reshape(B, S, 2 * A_HEADS, A_HD), pos).reshape(B, S, A_HEADS, 2, A_HD)
    v = v.reshape(B, S, A_HEADS, 2 * A_HD)
    lam_init = 0.8 - 0.6 * math.exp(-0.3 * layer_idx)
    lam = (jnp.exp(jnp.sum(lq1.astype(jnp.float32) * lk1.astype(jnp.float32)))
           - jnp.exp(jnp.sum(lq2.astype(jnp.float32) * lk2.astype(jnp.float32))) + lam_init)
    o = diff_attention(q, k, v, lam).astype(jnp.float32)
    o = o * lax.rsqrt(jnp.mean(jnp.square(o), axis=-1, keepdims=True) + LN_EPS)
    o = o * subln_g.astype(jnp.float32) * (1.0 - lam_init)
    y = o.reshape(B, S, A_V).astype(x.dtype) * jax.nn.silu(g)
    return y @ w_out


def dilated_attention(q, k, v, dil, radius):
    B, S, H, hd = q.shape
    L = S // dil
    nb = -(-L // B_BLK)
    Lp = nb * B_BLK
    scale = 1.0 / math.sqrt(hd)

    def to_res(t):
        return t.reshape(B, L, dil, H, hd).transpose(0, 2, 1, 3, 4)

    qr = jnp.pad(to_res(q), ((0, 0), (0, 0), (0, Lp - L), (0, 0), (0, 0)))
    qr = qr.reshape(B, dil, nb, B_BLK, H, hd)

    def windows(t):
        tp = jnp.pad(to_res(t), ((0, 0), (0, 0), (B_BLK, Lp - L + B_BLK), (0, 0), (0, 0)))
        tp = tp.reshape(B, dil, nb + 2, B_BLK, H, hd)
        return jnp.concatenate([tp[:, :, :-2], tp[:, :, 1:-1], tp[:, :, 2:]], axis=3)

    kw = windows(k)
    vw = windows(v)
    s = jnp.einsum('brnqhd,brnkhd->brnhqk', qr, kw).astype(jnp.float32) * scale
    qpos = jnp.arange(B_BLK)[:, None]
    t = jnp.arange(3 * B_BLK)[None, :]
    band = jnp.abs(t - B_BLK - qpos) <= radius
    kglob = jnp.arange(nb)[:, None] * B_BLK + jnp.arange(3 * B_BLK)[None, :] - B_BLK
    inrange = (kglob >= 0) & (kglob < L)
    mask = band[None, :, :] & inrange[:, None, :]
    s = jnp.where(mask[None, None, :, None, :, :], s, -1e30)
    lse = jax.nn.logsumexp(s, axis=-1)
    p = jnp.exp(s - lse[..., None])
    o = jnp.einsum('brnhqk,brnkhd->brnqhd', p.astype(v.dtype), vw)
    o = o.reshape(B, dil, Lp, H, hd)[:, :, :L].transpose(0, 2, 1, 3, 4).reshape(B, S, H, hd)
    lse = lse.transpose(0, 1, 2, 4, 3).reshape(B, dil, Lp, H)[:, :, :L]
    lse = lse.transpose(0, 2, 1, 3).reshape(B, S, H)
    return o, lse


def mixer_b(x, w_in, w_out):
    B, S, _ = x.shape
    h = x @ w_in
    parts = jnp.split(h, 3 * len(B_GROUPS) + 1, axis=-1)
    gate = parts[-1]
    pos = jnp.arange(S)
    outs, lses = [], []
    for gi, (win, dil) in enumerate(B_GROUPS):
        q, k, v = [p.reshape(B, S, B_HEADS, B_HD) for p in parts[3 * gi:3 * gi + 3]]
        q = rope_partial(q, pos)
        k = rope_partial(k, pos)
        o, lse = dilated_attention(q, k, v, dil, (win // 2) // dil)
        outs.append(o)
        lses.append(lse)
    wts = jax.nn.softmax(jnp.stack(lses, axis=0), axis=0)
    o = jnp.sum(wts[..., None] * jnp.stack(outs, axis=0).astype(jnp.float32), axis=0)
    y = o.reshape(B, S, B_W).astype(x.dtype) * jax.nn.silu(gate)
    return y @ w_out


def setup_inputs(seed: int = 0) -> dict:
    key = jax.random.key(seed)
    ks = iter(jax.random.split(key, 64))

    def nrm(shape, scale):
        return jax.random.normal(next(ks), shape, jnp.float32) * scale

    d = {}
    d["x_prompt"] = nrm((BATCH, SEQ, D_MODEL), 1.0)
    d["x_sample"] = nrm((DEC_BATCH, DEC_SEQ, D_MODEL), 1.0)
    for i in range(DEPTH):
        if i % 2 == 0:
            d[f"w_in_{i}"] = nrm((D_MODEL, 2 * A_QK + 2 * A_V), D_MODEL ** -0.5)
            d[f"lam_q1_{i}"] = nrm((A_HD,), 0.1)
            d[f"lam_k1_{i}"] = nrm((A_HD,), 0.1)
            d[f"lam_q2_{i}"] = nrm((A_HD,), 0.1)
            d[f"lam_k2_{i}"] = nrm((A_HD,), 0.1)
            d[f"subln_g_{i}"] = 1.0 + nrm((2 * A_HD,), 0.02)
            d[f"w_out_{i}"] = nrm((A_V, D_MODEL), BETA * A_V ** -0.5)
        else:
            d[f"w_in_{i}"] = nrm((D_MODEL, (3 * len(B_GROUPS) + 1) * B_W), D_MODEL ** -0.5)
            d[f"w_out_{i}"] = nrm((B_W, D_MODEL), BETA * B_W ** -0.5)
        d[f"ln_g_{i}"] = 1.0 + nrm((D_MODEL,), 0.02)
        d[f"ln_b_{i}"] = nrm((D_MODEL,), 0.02)
    return d


def reference(x_prompt, x_sample,
              w_in_0, lam_q1_0, lam_k1_0, lam_q2_0, lam_k2_0, subln_g_0, w_out_0, ln_g_0, ln_b_0,
              w_in_1, w_out_1, ln_g_1, ln_b_1,
              w_in_2, lam_q1_2, lam_k1_2, lam_q2_2, lam_k2_2, subln_g_2, w_out_2, ln_g_2, ln_b_2,
              w_in_3, w_out_3, ln_g_3, ln_b_3):
    layers = [
        ("A", (w_in_0, lam_q1_0, lam_k1_0, lam_q2_0, lam_k2_0, subln_g_0, w_out_0), ln_g_0, ln_b_0),
        ("B", (w_in_1, w_out_1), ln_g_1, ln_b_1),
        ("A", (w_in_2, lam_q1_2, lam_k1_2, lam_q2_2, lam_k2_2, subln_g_2, w_out_2), ln_g_2, ln_b_2),
        ("B", (w_in_3, w_out_3), ln_g_3, ln_b_3),
    ]

    def trunk(x):
        for i in range(DEPTH):
            kind, params, g, b = layers[i]
            if kind == "A":
                f = mixer_a(x, *params, layer_idx=i)
            else:
                f = mixer_b(x, *params)
            x = layer_norm(ALPHA * x + f, g, b)
        return x

    y_prompt = trunk(x_prompt)
    y_sample = trunk(x_sample)
    return (y_prompt, y_sample)
```

```python
import functools
import math

import jax
import jax.numpy as jnp
from jax import lax
from jax.experimental import pallas as pl
from jax.experimental.pallas import tpu as pltpu

F32 = jnp.float32
BF16 = jnp.bfloat16

D_MODEL = 1024
DEPTH = 4
ROPE_THETA = 500000.0
LN_EPS = 1e-5
ALPHA = (2.0 * DEPTH) ** 0.25
HEAD_DIM = 64
LANES = 128
N_PAIRS = D_MODEL // LANES
SECTION_PLANES = D_MODEL // LANES
B_GROUPS = ((128, 1), (512, 4), (2048, 16))
RADIUS = 64
TILE = 2048
HALO = 64
QBLK = 128
VMEM_LIMIT = 56 * 1024 * 1024


def _cparams(sem):
    return pltpu.CompilerParams(dimension_semantics=sem, vmem_limit_bytes=VMEM_LIMIT)


def _rope_tables(seq):
    half = HEAD_DIM // 8
    inv = ROPE_THETA ** (-jnp.arange(half, dtype=F32) / half)
    ang = jnp.arange(seq, dtype=F32)[:, None] * inv[None, :]
    cos, sin = jnp.cos(ang), jnp.sin(ang)
    ones = jnp.ones((seq, HEAD_DIM - 2 * half), F32)
    zeros8 = jnp.zeros((seq, half), F32)
    zeros_rest = jnp.zeros((seq, HEAD_DIM - 2 * half), F32)
    c = jnp.concatenate([cos, cos, ones], axis=1)
    sa = jnp.concatenate([-sin, zeros8, zeros_rest], axis=1)
    sb = jnp.concatenate([zeros8, sin, zeros_rest], axis=1)

    return tuple(jnp.concatenate([t, t], axis=1) for t in (c, sa, sb))


def _proj_kernel(x_ref, w_ref, c_ref, sa_ref, sb_ref, o_ref, xs_ref, tmp_ref, *, dil, tm, tn, rc, n_rope_tiles):
    j = pl.program_id(1)

    @pl.when(j == 0)
    def _():
        xs_ref[...] = x_ref[...].astype(BF16)

    res_rows = tm // dil
    sub = rc // dil

    def run(rope):
        def body(ci, carry):
            rows = pl.ds(pl.multiple_of(ci * rc, rc), rc)
            acc = jnp.dot(xs_ref[rows, :], w_ref[...], preferred_element_type=F32)
            if rope:
                c, sa, sb = c_ref[rows, :], sa_ref[rows, :], sb_ref[rows, :]
            for t in range(tn // LANES):
                a = acc[:, t * LANES:(t + 1) * LANES]
                if rope:
                    a = a * c + pltpu.roll(a, LANES - 8, 1) * sa + pltpu.roll(a, 8, 1) * sb
                if dil == 1:
                    o_ref[t, rows, :] = a.astype(BF16)
                else:
                    tmp_ref[...] = a
                    for r in range(dil):
                        dst = pl.ds(pl.multiple_of(r * res_rows + ci * sub, sub), sub)
                        o_ref[t, dst, :] = tmp_ref[pl.ds(r, sub, stride=dil), :].astype(BF16)
            return carry

        lax.fori_loop(0, tm // rc, body, 0)

    @pl.when(j < n_rope_tiles)
    def _():
        run(True)

    @pl.when(j >= n_rope_tiles)
    def _():
        run(False)


def _project(x2d, w, tabs, *, seq, dil, tm, tn=512, rc=256):
    n = x2d.shape[0]
    n_out = w.shape[1]
    n_seq_tiles = seq // tm
    kern = functools.partial(_proj_kernel, dil=dil, tm=tm, tn=tn, rc=rc, n_rope_tiles=2 * D_MODEL // tn)
    tab_spec = pl.BlockSpec((tm, LANES), lambda i, j: (i % n_seq_tiles, 0))
    return pl.pallas_call(
        kern,
        grid=(n // tm, n_out // tn),
        in_specs=[
            pl.BlockSpec((tm, D_MODEL), lambda i, j: (i, 0)),
            pl.BlockSpec((D_MODEL, tn), lambda i, j: (0, j)),
            tab_spec, tab_spec, tab_spec,
        ],
        out_specs=pl.BlockSpec((tn // LANES, tm, LANES), lambda i, j: (j, i, 0)),
        out_shape=jax.ShapeDtypeStruct((n_out // LANES, n, LANES), BF16),
        scratch_shapes=[pltpu.VMEM((tm, D_MODEL), BF16), pltpu.VMEM((rc, LANES), F32)],
        compiler_params=_cparams(("arbitrary", "arbitrary")),
    )(x2d, w, *tabs)


def _silu(g):
    return g * (1.0 / (1.0 + jnp.exp(-g)))


def _attn_a_kernel(lamv_ref, subg_ref, q_ref, k_ref, v_ref, g_ref, o_ref,
                   qs_ref, m_ref, l_ref, acc_ref, *, lam_init, bq, bk):
    kv = pl.program_id(3)

    @pl.when(kv == 0)
    def _():
        q = q_ref[...]
        lane = lax.broadcasted_iota(jnp.int32, q.shape, 1)
        zero = jnp.zeros_like(q)
        qs_ref[0:bq, :] = jnp.where(lane < HEAD_DIM, q, zero)
        qs_ref[bq:2 * bq, :] = jnp.where(lane >= HEAD_DIM, q, zero)
        m_ref[...] = jnp.full(m_ref.shape, -jnp.inf, F32)
        l_ref[...] = jnp.zeros(l_ref.shape, F32)
        acc_ref[...] = jnp.zeros(acc_ref.shape, F32)

    s = lax.dot_general(qs_ref[...], k_ref[...], (((1,), (1,)), ((), ())), preferred_element_type=F32)
    m_prev = m_ref[...]
    m_new = jnp.maximum(m_prev, jnp.max(s, axis=1, keepdims=True))
    alpha = jnp.exp(m_prev - m_new)
    p = jnp.exp(s - pltpu.repeat(m_new, bk // LANES, 1))
    l_ref[...] = alpha * l_ref[...] + jnp.sum(p, axis=1, keepdims=True)
    acc_ref[...] = alpha * acc_ref[...] + jnp.dot(p.astype(BF16), v_ref[...], preferred_element_type=F32)
    m_ref[...] = m_new

    @pl.when(kv == pl.num_programs(3) - 1)
    def _():
        lv = lamv_ref[...]
        lam = (jnp.exp(jnp.sum(lv[0:1] * lv[1:2], axis=1, keepdims=True))
               - jnp.exp(jnp.sum(lv[2:3] * lv[3:4], axis=1, keepdims=True)) + lam_init)
        o = acc_ref[...] / l_ref[...]
        od = o[0:bq] - lam * o[bq:2 * bq]
        od = od * lax.rsqrt(jnp.mean(od * od, axis=1, keepdims=True) + LN_EPS)
        od = od * subg_ref[...] * (1.0 - lam_init)
        o_ref[...] = (od * _silu(g_ref[...].astype(F32))).astype(BF16)


def _attn_a(planes, lamv, subg, *, batch, seq, lam_init, bq=512, bk=512):
    pl4 = planes.reshape(4 * SECTION_PLANES, batch, seq, LANES)
    kern = functools.partial(_attn_a_kernel, lam_init=lam_init, bq=bq, bk=bk)

    def sec(s, rows_from_q):
        if rows_from_q:
            return pl.BlockSpec((None, None, bq, LANES), lambda b, h, qi, ki: (s * SECTION_PLANES + h, b, qi, 0))
        return pl.BlockSpec((None, None, bk, LANES), lambda b, h, qi, ki: (s * SECTION_PLANES + h, b, ki, 0))

    return pl.pallas_call(
        kern,
        grid=(batch, N_PAIRS, seq // bq, seq // bk),
        in_specs=[
            pl.BlockSpec((4, HEAD_DIM), lambda b, h, qi, ki: (0, 0)),
            pl.BlockSpec((1, LANES), lambda b, h, qi, ki: (0, 0)),
            sec(0, True), sec(1, False), sec(2, False), sec(3, True),
        ],
        out_specs=pl.BlockSpec((None, bq, LANES), lambda b, h, qi, ki: (b, qi, h)),
        out_shape=jax.ShapeDtypeStruct((batch, seq, D_MODEL), BF16),
        scratch_shapes=[
            pltpu.VMEM((2 * bq, LANES), BF16),
            pltpu.VMEM((2 * bq, LANES), F32),
            pltpu.VMEM((2 * bq, LANES), F32),
            pltpu.VMEM((2 * bq, LANES), F32),
        ],
        compiler_params=_cparams(("arbitrary", "arbitrary", "arbitrary", "arbitrary")),
    )(lamv, subg, pl4, pl4, pl4, pl4)


def _attn_b_group(gi, dil, seq, q_ref, k_ref, kp_ref, kn_ref, v_ref, vp_ref, vn_ref,
                  kw_ref, vw_ref, og_ref, lg_ref):
    rows = TILE // dil
    nb64 = rows // HALO
    tile_i = pl.program_id(2)
    length = seq // dil

    kw_ref[0:dil, 0:HALO, :] = kp_ref[...]
    vw_ref[0:dil, 0:HALO, :] = vp_ref[...]
    for n in range(nb64):
        kw_ref[0:dil, HALO + n * HALO:2 * HALO + n * HALO, :] = k_ref[:, n]
        vw_ref[0:dil, HALO + n * HALO:2 * HALO + n * HALO, :] = v_ref[:, n]
    kw_ref[0:dil, HALO + rows:2 * HALO + rows, :] = kn_ref[...]
    vw_ref[0:dil, HALO + rows:2 * HALO + rows, :] = vn_ref[...]

    win = QBLK + 2 * HALO
    row_i = lax.broadcasted_iota(jnp.int32, (2 * QBLK, win), 0)
    col_i = lax.broadcasted_iota(jnp.int32, (2 * QBLK, win), 1)
    qpos = jnp.where(row_i >= QBLK, row_i - QBLK, row_i)
    band = jnp.abs(col_i - HALO - qpos) <= RADIUS
    lane_q = lax.broadcasted_iota(jnp.int32, (QBLK, LANES), 1)
    lo_q = lane_q < HEAD_DIM

    def per_residue(r, carry):
        for blk in range(rows // QBLK):
            q2 = q_ref[r, 2 * blk:2 * blk + 2].reshape(QBLK, LANES)
            zero = jnp.zeros_like(q2)
            qs = jnp.concatenate([jnp.where(lo_q, q2, zero), jnp.where(lo_q, zero, q2)], axis=0)
            kwin = kw_ref[r, blk * QBLK:blk * QBLK + win, :]
            vwin = vw_ref[r, blk * QBLK:blk * QBLK + win, :]
            s = lax.dot_general(qs, kwin, (((1,), (1,)), ((), ())), preferred_element_type=F32)
            kglob = tile_i * rows + (blk * QBLK - HALO) + col_i
            mask = band & (kglob >= 0) & (kglob < length)
            s = jnp.where(mask, s, -1e30)
            m = jnp.max(s, axis=1, keepdims=True)
            p = jnp.exp(s - m)
            l = jnp.sum(p, axis=1, keepdims=True)
            lse = m + jnp.log(l)
            pb = p.astype(BF16)
            o0 = jnp.dot(pb[0:QBLK], vwin, preferred_element_type=F32) / l[0:QBLK]
            o1 = jnp.dot(pb[QBLK:2 * QBLK], vwin, preferred_element_type=F32) / l[QBLK:2 * QBLK]
            o = jnp.where(lo_q, o0, o1)
            lse2 = jnp.where(lo_q, jnp.broadcast_to(lse[0:QBLK], (QBLK, LANES)),
                             jnp.broadcast_to(lse[QBLK:2 * QBLK], (QBLK, LANES)))
            tok = pl.ds(r + blk * QBLK * dil, QBLK, stride=dil) if dil > 1 else pl.ds(blk * QBLK, QBLK)
            og_ref[gi, tok, :] = o
            lg_ref[gi, tok, :] = lse2
        return carry

    lax.fori_loop(0, dil, per_residue, 0)


def _attn_b_kernel(*refs, seq):
    n_in = 7 * len(B_GROUPS) + 1
    gate_ref = refs[n_in - 1]
    o_ref = refs[n_in]
    kw_ref, vw_ref, og_ref, lg_ref = refs[n_in + 1:]
    for gi, (_, dil) in enumerate(B_GROUPS):
        _attn_b_group(gi, dil, seq, *refs[7 * gi:7 * gi + 7], kw_ref, vw_ref, og_ref, lg_ref)
    l0, l1, l2 = lg_ref[0], lg_ref[1], lg_ref[2]
    mx = jnp.maximum(jnp.maximum(l0, l1), l2)
    e0, e1, e2 = jnp.exp(l0 - mx), jnp.exp(l1 - mx), jnp.exp(l2 - mx)
    den = e0 + e1 + e2
    o = (e0 / den) * og_ref[0] + (e1 / den) * og_ref[1] + (e2 / den) * og_ref[2]
    o_ref[...] = (o * _silu(gate_ref[...].astype(F32))).astype(BF16)


def _attn_b(group_planes, gate_planes, gate_base, *, batch, seq):
    n_tiles = seq // TILE
    in_specs, args = [], []
    for (_, dil), planes in zip(B_GROUPS, group_planes):
        rows = TILE // dil
        nb64 = rows // HALO
        view = planes.reshape(3 * SECTION_PLANES, batch, n_tiles, dil, nb64, HALO, LANES)

        def main(s):
            return pl.BlockSpec((None, None, None, dil, nb64, HALO, LANES),
                                lambda b, h, i, s=s: (s * SECTION_PLANES + h, b, i, 0, 0, 0, 0))

        def prev(s, nb64=nb64):
            return pl.BlockSpec((None, None, None, dil, None, HALO, LANES),
                                lambda b, h, i, s=s: (s * SECTION_PLANES + h, b, jnp.maximum(i - 1, 0), 0, nb64 - 1, 0, 0))

        def nxt(s):
            return pl.BlockSpec((None, None, None, dil, None, HALO, LANES),
                                lambda b, h, i, s=s: (s * SECTION_PLANES + h, b, jnp.minimum(i + 1, n_tiles - 1), 0, 0, 0, 0))

        in_specs += [main(0), main(1), prev(1), nxt(1), main(2), prev(2), nxt(2)]
        args += [view] * 7
    gate_view = gate_planes.reshape(gate_planes.shape[0], batch, seq, LANES)
    in_specs.append(pl.BlockSpec((None, None, TILE, LANES), lambda b, h, i: (gate_base + h, b, i, 0)))
    args.append(gate_view)
    max_dil = max(d for _, d in B_GROUPS)
    return pl.pallas_call(
        functools.partial(_attn_b_kernel, seq=seq),
        grid=(batch, N_PAIRS, n_tiles),
        in_specs=in_specs,
        out_specs=pl.BlockSpec((None, TILE, LANES), lambda b, h, i: (b, i, h)),
        out_shape=jax.ShapeDtypeStruct((batch, seq, D_MODEL), BF16),
        scratch_shapes=[
            pltpu.VMEM((max_dil, TILE + 2 * HALO, LANES), BF16),
            pltpu.VMEM((max_dil, TILE + 2 * HALO, LANES), BF16),
            pltpu.VMEM((len(B_GROUPS), TILE, LANES), F32),
            pltpu.VMEM((len(B_GROUPS), TILE, LANES), F32),
        ],
        compiler_params=_cparams(("arbitrary", "arbitrary", "arbitrary")),
    )(*args)


def _out_ln_kernel(y_ref, w_ref, x_ref, g_ref, b_ref, o_ref, *, tm, rc):
    def body(ci, carry):
        rows = pl.ds(pl.multiple_of(ci * rc, rc), rc)
        f = jnp.dot(y_ref[rows, :], w_ref[...], preferred_element_type=F32)
        z = ALPHA * x_ref[rows, :] + f
        mu = jnp.mean(z, axis=1, keepdims=True)
        zc = z - mu
        var = jnp.mean(zc * zc, axis=1, keepdims=True)
        o_ref[rows, :] = zc * lax.rsqrt(var + LN_EPS) * g_ref[...] + b_ref[...]
        return carry

    lax.fori_loop(0, tm // rc, body, 0)


def _out_ln(y2d, w, x2d, g, b, *, tm=512, rc=256):
    n = x2d.shape[0]
    return pl.pallas_call(
        functools.partial(_out_ln_kernel, tm=tm, rc=rc),
        grid=(n // tm,),
        in_specs=[
            pl.BlockSpec((tm, D_MODEL), lambda i: (i, 0)),
            pl.BlockSpec((D_MODEL, D_MODEL), lambda i: (0, 0)),
            pl.BlockSpec((tm, D_MODEL), lambda i: (i, 0)),
            pl.BlockSpec((1, D_MODEL), lambda i: (0, 0)),
            pl.BlockSpec((1, D_MODEL), lambda i: (0, 0)),
        ],
        out_specs=pl.BlockSpec((tm, D_MODEL), lambda i: (i, 0)),
        out_shape=jax.ShapeDtypeStruct((n, D_MODEL), F32),
        compiler_params=_cparams(("arbitrary",)),
    )(y2d, w, x2d, g.reshape(1, D_MODEL), b.reshape(1, D_MODEL))


def _scale_q(w, q_cols):
    scale = jnp.ones((w.shape[1],), F32)
    for lo in q_cols:
        scale = scale.at[lo:lo + D_MODEL].set(1.0 / math.sqrt(HEAD_DIM))
    return (w * scale[None, :]).astype(BF16)


def _layer_a(x, p, layer_idx):
    batch, seq, _ = x.shape
    x2d = x.reshape(batch * seq, D_MODEL)
    w_in = _scale_q(p["w_in"], (0,))
    tabs = _rope_tables(seq)
    planes = _project(x2d, w_in, tabs, seq=seq, dil=1, tm=1024)
    lam_init = 0.8 - 0.6 * math.exp(-0.3 * layer_idx)
    lamv = jnp.stack([p["lam_q1"], p["lam_k1"], p["lam_q2"], p["lam_k2"]]).astype(F32)
    y = _attn_a(planes, lamv, p["subln_g"].astype(F32).reshape(1, LANES), batch=batch, seq=seq, lam_init=lam_init)
    out = _out_ln(y.reshape(batch * seq, D_MODEL), p["w_out"].astype(BF16), x2d, p["ln_g"], p["ln_b"])
    return out.reshape(batch, seq, D_MODEL)


def _layer_b(x, p):
    batch, seq, _ = x.shape
    x2d = x.reshape(batch * seq, D_MODEL)
    w_in = p["w_in"]
    tabs = _rope_tables(seq)
    group_planes = []
    gate_planes = None
    for gi, (_, dil) in enumerate(B_GROUPS):
        w = w_in[:, 3 * gi * D_MODEL:3 * (gi + 1) * D_MODEL]
        if dil == 1:
            w = jnp.concatenate([w, w_in[:, 3 * len(B_GROUPS) * D_MODEL:]], axis=1)
        planes = _project(x2d, _scale_q(w, (0,)), tabs, seq=seq, dil=dil, tm=TILE)
        if dil == 1:
            gate_planes = planes
        group_planes.append(planes[:3 * SECTION_PLANES] if dil == 1 else planes)
    y = _attn_b(group_planes, gate_planes, 3 * SECTION_PLANES, batch=batch, seq=seq)
    out = _out_ln(y.reshape(batch * seq, D_MODEL), p["w_out"].astype(BF16), x2d, p["ln_g"], p["ln_b"])
    return out.reshape(batch, seq, D_MODEL)


def _trunk(x, layers):
    for i, p in enumerate(layers):
        x = _layer_a(x, p, i) if i % 2 == 0 else _layer_b(x, p)
    return x


def kernel(x_prompt, x_sample, w_in_0, lam_q1_0, lam_k1_0, lam_q2_0, lam_k2_0, subln_g_0, w_out_0, ln_g_0, ln_b_0, w_in_1, w_out_1, ln_g_1, ln_b_1, w_in_2, lam_q1_2, lam_k1_2, lam_q2_2, lam_k2_2, subln_g_2, w_out_2, ln_g_2, ln_b_2, w_in_3, w_out_3, ln_g_3, ln_b_3):
    layers = [
        dict(w_in=w_in_0, lam_q1=lam_q1_0, lam_k1=lam_k1_0, lam_q2=lam_q2_0, lam_k2=lam_k2_0,
             subln_g=subln_g_0, w_out=w_out_0, ln_g=ln_g_0, ln_b=ln_b_0),
        dict(w_in=w_in_1, w_out=w_out_1, ln_g=ln_g_1, ln_b=ln_b_1),
        dict(w_in=w_in_2, lam_q1=lam_q1_2, lam_k1=lam_k1_2, lam_q2=lam_q2_2, lam_k2=lam_k2_2,
             subln_g=subln_g_2, w_out=w_out_2, ln_g=ln_g_2, ln_b=ln_b_2),
        dict(w_in=w_in_3, w_out=w_out_3, ln_g=ln_g_3, ln_b=ln_b_3),
    ]
    return (_trunk(x_prompt, layers), _trunk(x_sample, layers))
```

```python
import functools
import math

import jax
import jax.numpy as jnp
from jax import lax
from jax.experimental import pallas as pl
from jax.experimental.pallas import tpu as pltpu

F32 = jnp.float32
BF16 = jnp.bfloat16

D_MODEL = 1024
DEPTH = 4
ROPE_THETA = 500000.0
LN_EPS = 1e-5
ALPHA = (2.0 * DEPTH) ** 0.25
HEAD_DIM = 64
LANES = 128
N_PAIRS = D_MODEL // LANES
SECTION_PLANES = D_MODEL // LANES
B_GROUPS = ((128, 1), (512, 4), (2048, 16))
RADIUS = 64
TILE = 2048
HALO = 64
QBLK = 128
MASKED = -1e30
VMEM_LIMIT = 56 * 1024 * 1024


def _cparams(sem):
    return pltpu.CompilerParams(dimension_semantics=sem, vmem_limit_bytes=VMEM_LIMIT)


def _rope_tables(seq):
    half = HEAD_DIM // 8
    inv = ROPE_THETA ** (-jnp.arange(half, dtype=F32) / half)
    ang = jnp.arange(seq, dtype=F32)[:, None] * inv[None, :]
    cos, sin = jnp.cos(ang), jnp.sin(ang)
    ones = jnp.ones((seq, HEAD_DIM - 2 * half), F32)
    zeros8 = jnp.zeros((seq, half), F32)
    zeros_rest = jnp.zeros((seq, HEAD_DIM - 2 * half), F32)
    c = jnp.concatenate([cos, cos, ones], axis=1)
    sa = jnp.concatenate([-sin, zeros8, zeros_rest], axis=1)
    sb = jnp.concatenate([zeros8, sin, zeros_rest], axis=1)

    return tuple(jnp.concatenate([t, t], axis=1) for t in (c, sa, sb))


X_SLABS = D_MODEL // LANES


def _proj_kernel(*refs, dil, tm, tn, sub_n, n_rope_tiles):
    x_refs = refs[:X_SLABS]
    w_ref, c_ref, sa_ref, sb_ref, o_ref, xs_ref = refs[X_SLABS:]
    j = pl.program_id(1)
    res_rows = tm // dil

    @pl.when(j == 0)
    def _():
        for c in range(X_SLABS):
            for r in range(dil):
                src = x_refs[c][pl.ds(r, res_rows, stride=dil), :] if dil > 1 else x_refs[c][...]
                xs_ref[r * res_rows:(r + 1) * res_rows, c * LANES:(c + 1) * LANES] = src.astype(BF16)

    def run(rope):
        for sd in range(tn // sub_n):
            acc = jnp.dot(xs_ref[...], w_ref[:, sd * sub_n:(sd + 1) * sub_n], preferred_element_type=F32)
            for t in range(sub_n // LANES):
                a = acc[:, t * LANES:(t + 1) * LANES]
                if rope:
                    a = a * c_ref[...] + pltpu.roll(a, LANES - 8, 1) * sa_ref[...] + pltpu.roll(a, 8, 1) * sb_ref[...]
                ab = a.astype(BF16)
                for r in range(dil):
                    o_ref[sd * (sub_n // LANES) + t, r] = ab[r * res_rows:(r + 1) * res_rows]

    @pl.when(j < n_rope_tiles)
    def _():
        run(True)

    @pl.when(j >= n_rope_tiles)
    def _():
        run(False)


def _project(x2d, w, tabs, *, seq, dil, tile, tm=1024, tn=1024, sub_n=512):
    n = x2d.shape[0]
    n_out = w.shape[1]
    per_tile = tile // tm
    n_seq_tiles = seq // tm
    if dil > 1:
        tabs = tuple(t.reshape(seq // tm, tm // dil, dil, LANES).transpose(0, 2, 1, 3).reshape(seq, LANES)
                     for t in tabs)
    kern = functools.partial(_proj_kernel, dil=dil, tm=tm, tn=tn, sub_n=sub_n, n_rope_tiles=2 * D_MODEL // tn)
    tab_spec = pl.BlockSpec((tm, LANES), lambda i, j: (i % n_seq_tiles, 0))
    x_specs = [pl.BlockSpec((tm, LANES), lambda i, j, c=c: (i, c)) for c in range(X_SLABS)]
    return pl.pallas_call(
        kern,
        grid=(n // tm, n_out // tn),
        in_specs=x_specs + [
            pl.BlockSpec((D_MODEL, tn), lambda i, j: (0, j)),
            tab_spec, tab_spec, tab_spec,
        ],
        out_specs=pl.BlockSpec((tn // LANES, None, dil, tm // dil, LANES),
                               lambda i, j: (j, i // per_tile, 0, i % per_tile, 0)),
        out_shape=jax.ShapeDtypeStruct((n_out // LANES, n // tile, dil, tile // dil, LANES), BF16),
        scratch_shapes=[pltpu.VMEM((tm, D_MODEL), BF16)],
        compiler_params=_cparams(("arbitrary", "arbitrary")),
    )(*([x2d] * X_SLABS), w, *tabs)


def _silu(g):
    return g * (1.0 / (1.0 + jnp.exp(-g)))


def _attn_a_kernel(lamv_ref, subg_ref, q_ref, k_ref, v_ref, g_ref, o_ref,
                   qs_ref, sa_ref, sb_ref, m_ref, l_ref, acc_ref, *, lam_init, bq, bk, nkv, rchunk):
    q = q_ref[...]
    lane = lax.broadcasted_iota(jnp.int32, q.shape, 1)
    zero = jnp.zeros_like(q)
    qs_ref[0:bq, :] = jnp.where(lane < HEAD_DIM, q, zero)
    qs_ref[bq:2 * bq, :] = jnp.where(lane >= HEAD_DIM, q, zero)
    m_ref[...] = jnp.full(m_ref.shape, -jnp.inf, F32)
    l_ref[...] = jnp.zeros(l_ref.shape, F32)
    acc_ref[...] = jnp.zeros(acc_ref.shape, F32)

    def scores(j, s_ref):
        kc = k_ref[pl.ds(pl.multiple_of(j * bk, bk), bk), :]
        s_ref[...] = lax.dot_general(qs_ref[...], kc, (((1,), (1,)), ((), ())), preferred_element_type=F32)

    def update(j, s_ref):
        vc = v_ref[pl.ds(pl.multiple_of(j * bk, bk), bk), :]
        for r0 in range(0, 2 * bq, rchunk):
            rows = slice(r0, r0 + rchunk)
            m_prev = m_ref[rows, :]
            m_new = jnp.maximum(m_prev, jnp.max(s_ref[rows, :], axis=1, keepdims=True))
            alpha = jnp.exp2(m_prev - m_new)
            p = jnp.exp2(s_ref[rows, :] - pltpu.repeat(m_new, bk // LANES, 1))
            psum = p[:, 0:LANES]
            for t in range(1, bk // LANES):
                psum = psum + p[:, t * LANES:(t + 1) * LANES]
            l_ref[rows, :] = alpha * l_ref[rows, :] + psum
            acc_ref[rows, :] = alpha * acc_ref[rows, :] + jnp.dot(p.astype(BF16), vc, preferred_element_type=F32)
            m_ref[rows, :] = m_new

    scores(0, sa_ref)

    def pair(i, carry):
        scores(2 * i + 1, sb_ref)
        update(2 * i, sa_ref)
        scores(2 * i + 2, sa_ref)
        update(2 * i + 1, sb_ref)
        return carry

    for i in range((nkv - 2) // 2):
        pair(i, 0)
    scores(nkv - 1, sb_ref)
    update(nkv - 2, sa_ref)
    update(nkv - 1, sb_ref)

    lv = lamv_ref[...]
    lam = (jnp.exp(jnp.sum(lv[0:1] * lv[1:2], axis=1, keepdims=True))
           - jnp.exp(jnp.sum(lv[2:3] * lv[3:4], axis=1, keepdims=True)) + lam_init)
    o = acc_ref[...] / jnp.sum(l_ref[...], axis=1, keepdims=True)
    od = o[0:bq] - lam * o[bq:2 * bq]
    od = od * lax.rsqrt(jnp.mean(od * od, axis=1, keepdims=True) + LN_EPS)
    od = od * subg_ref[...] * (1.0 - lam_init)
    o_ref[...] = (od * _silu(g_ref[...].astype(F32))).astype(BF16)


def _attn_a(planes, lamv, subg, *, batch, seq, lam_init, bq=512, bk=1024, rchunk=256):
    pl4 = planes.reshape(4 * SECTION_PLANES, batch, seq, LANES)
    nkv = seq // bk
    assert nkv >= 2 and nkv % 2 == 0 and (2 * bq) % rchunk == 0
    kern = functools.partial(_attn_a_kernel, lam_init=lam_init, bq=bq, bk=bk, nkv=nkv, rchunk=rchunk)

    def q_rows(s):
        return pl.BlockSpec((None, None, bq, LANES), lambda b, h, qi: (s * SECTION_PLANES + h, b, qi, 0))

    def all_rows(s):
        return pl.BlockSpec((None, None, seq, LANES), lambda b, h, qi: (s * SECTION_PLANES + h, b, 0, 0))

    return pl.pallas_call(
        kern,
        grid=(batch, N_PAIRS, seq // bq),
        in_specs=[
            pl.BlockSpec((4, HEAD_DIM), lambda b, h, qi: (0, 0)),
            pl.BlockSpec((1, LANES), lambda b, h, qi: (0, 0)),
            q_rows(0), all_rows(1), all_rows(2), q_rows(3),
        ],
        out_specs=pl.BlockSpec((None, bq, LANES), lambda b, h, qi: (b, qi, h)),
        out_shape=jax.ShapeDtypeStruct((batch, seq, D_MODEL), BF16),
        scratch_shapes=[
            pltpu.VMEM((2 * bq, LANES), BF16),
            pltpu.VMEM((2 * bq, bk), F32),
            pltpu.VMEM((2 * bq, bk), F32),
            pltpu.VMEM((2 * bq, LANES), F32),
            pltpu.VMEM((2 * bq, LANES), F32),
            pltpu.VMEM((2 * bq, LANES), F32),
        ],
        compiler_params=_cparams(("arbitrary", "arbitrary", "arbitrary")),
    )(lamv, subg, pl4, pl4, pl4, pl4)


def _attn_b_group(gi, dil, seq, q_ref, k_ref, kp_ref, kn_ref, v_ref, vp_ref, vn_ref,
                  kw_ref, vw_ref, og_ref, lg_ref, bias_ref):
    rows = TILE // dil
    nb64 = rows // HALO
    tile_i = pl.program_id(2)
    length = seq // dil

    kw_ref[0:dil, 0:HALO, :] = kp_ref[...]
    vw_ref[0:dil, 0:HALO, :] = vp_ref[...]
    for n in range(nb64):
        kw_ref[0:dil, HALO + n * HALO:2 * HALO + n * HALO, :] = k_ref[:, n]
        vw_ref[0:dil, HALO + n * HALO:2 * HALO + n * HALO, :] = v_ref[:, n]
    kw_ref[0:dil, HALO + rows:2 * HALO + rows, :] = kn_ref[...]
    vw_ref[0:dil, HALO + rows:2 * HALO + rows, :] = vn_ref[...]

    win = QBLK + 2 * HALO
    col_i = lax.broadcasted_iota(jnp.int32, (1, win), 1)
    low_row = jnp.where(col_i < HALO, MASKED, 0.0)
    high_row = jnp.where(col_i >= QBLK + HALO, MASKED, 0.0)
    lane_q = lax.broadcasted_iota(jnp.int32, (QBLK, LANES), 1)
    lo_q = lane_q < HEAD_DIM
    blocks = rows // QBLK
    unroll = max(1, 4 // blocks)

    def block(r, blk):
        q2 = q_ref[r, 2 * blk:2 * blk + 2].reshape(QBLK, LANES)
        zero = jnp.zeros_like(q2)
        qs = jnp.concatenate([jnp.where(lo_q, q2, zero), jnp.where(lo_q, zero, q2)], axis=0)
        kwin = kw_ref[r, blk * QBLK:blk * QBLK + win, :]
        vwin = vw_ref[r, blk * QBLK:blk * QBLK + win, :]
        s = lax.dot_general(qs, kwin, (((1,), (1,)), ((), ())), preferred_element_type=F32)
        first = tile_i * rows + blk * QBLK
        edge = (jnp.where(first == 0, low_row, 0.0) + jnp.where(first + QBLK == length, high_row, 0.0))
        s = s + bias_ref[...] + edge
        m = jnp.max(s, axis=1, keepdims=True)
        p = jnp.exp2(s - m)
        l = jnp.sum(p, axis=1, keepdims=True)
        lse = m + jnp.log2(l)
        pb = p.astype(BF16)
        o0 = jnp.dot(pb[0:QBLK], vwin, preferred_element_type=F32) / l[0:QBLK]
        o1 = jnp.dot(pb[QBLK:2 * QBLK], vwin, preferred_element_type=F32) / l[QBLK:2 * QBLK]
        o = jnp.where(lo_q, o0, o1)
        lse2 = jnp.where(lo_q, jnp.broadcast_to(lse[0:QBLK], (QBLK, LANES)),
                         jnp.broadcast_to(lse[QBLK:2 * QBLK], (QBLK, LANES)))
        tok = pl.ds(r + blk * QBLK * dil, QBLK, stride=dil) if dil > 1 else pl.ds(blk * QBLK, QBLK)
        og_ref[gi, tok, :] = o
        lg_ref[gi, tok, :] = lse2

    def body(it, carry):
        for u in range(unroll):
            for blk in range(blocks):
                block(it * unroll + u, blk)
        return carry

    if dil == 1:
        body(0, 0)
    else:
        lax.fori_loop(0, dil // unroll, body, 0)


def _attn_b_kernel(*refs, seq):
    n_in = 7 * len(B_GROUPS) + 1
    gate_ref = refs[n_in - 1]
    o_ref = refs[n_in]
    kw_ref, vw_ref, og_ref, lg_ref, bias_ref = refs[n_in + 1:]

    win = QBLK + 2 * HALO
    row_i = lax.broadcasted_iota(jnp.int32, (2 * QBLK, win), 0)
    col_i = lax.broadcasted_iota(jnp.int32, (2 * QBLK, win), 1)
    qpos = jnp.where(row_i >= QBLK, row_i - QBLK, row_i)
    bias_ref[...] = jnp.where(jnp.abs(col_i - HALO - qpos) <= RADIUS, 0.0, MASKED)

    for gi, (_, dil) in enumerate(B_GROUPS):
        _attn_b_group(gi, dil, seq, *refs[7 * gi:7 * gi + 7], kw_ref, vw_ref, og_ref, lg_ref, bias_ref)
    l0, l1, l2 = lg_ref[0], lg_ref[1], lg_ref[2]
    mx = jnp.maximum(jnp.maximum(l0, l1), l2)
    e0, e1, e2 = jnp.exp2(l0 - mx), jnp.exp2(l1 - mx), jnp.exp2(l2 - mx)
    den = e0 + e1 + e2
    o = (e0 / den) * og_ref[0] + (e1 / den) * og_ref[1] + (e2 / den) * og_ref[2]
    o_ref[...] = (o * _silu(gate_ref[...].astype(F32))).astype(BF16)


def _attn_b(group_planes, gate_planes, gate_base, *, batch, seq):
    n_tiles = seq // TILE
    in_specs, args = [], []
    for (_, dil), planes in zip(B_GROUPS, group_planes):
        rows = TILE // dil
        nb64 = rows // HALO
        view = planes.reshape(planes.shape[0], batch, n_tiles, dil, nb64, HALO, LANES)

        def main(s):
            return pl.BlockSpec((None, None, None, dil, nb64, HALO, LANES),
                                lambda b, h, i, s=s: (s * SECTION_PLANES + h, b, i, 0, 0, 0, 0))

        def prev(s, nb64=nb64):
            return pl.BlockSpec((None, None, None, dil, None, HALO, LANES),
                                lambda b, h, i, s=s: (s * SECTION_PLANES + h, b, jnp.maximum(i - 1, 0), 0, nb64 - 1, 0, 0))

        def nxt(s):
            return pl.BlockSpec((None, None, None, dil, None, HALO, LANES),
                                lambda b, h, i, s=s: (s * SECTION_PLANES + h, b, jnp.minimum(i + 1, n_tiles - 1), 0, 0, 0, 0))

        in_specs += [main(0), main(1), prev(1), nxt(1), main(2), prev(2), nxt(2)]
        args += [view] * 7
    gate_view = gate_planes.reshape(gate_planes.shape[0], batch, seq, LANES)
    in_specs.append(pl.BlockSpec((None, None, TILE, LANES), lambda b, h, i: (gate_base + h, b, i, 0)))
    args.append(gate_view)
    max_dil = max(d for _, d in B_GROUPS)
    return pl.pallas_call(
        functools.partial(_attn_b_kernel, seq=seq),
        grid=(batch, N_PAIRS, n_tiles),
        in_specs=in_specs,
        out_specs=pl.BlockSpec((None, TILE, LANES), lambda b, h, i: (b, i, h)),
        out_shape=jax.ShapeDtypeStruct((batch, seq, D_MODEL), BF16),
        scratch_shapes=[
            pltpu.VMEM((max_dil, TILE + 2 * HALO, LANES), BF16),
            pltpu.VMEM((max_dil, TILE + 2 * HALO, LANES), BF16),
            pltpu.VMEM((len(B_GROUPS), TILE, LANES), F32),
            pltpu.VMEM((len(B_GROUPS), TILE, LANES), F32),
            pltpu.VMEM((2 * QBLK, QBLK + 2 * HALO), F32),
        ],
        compiler_params=_cparams(("arbitrary", "arbitrary", "arbitrary")),
    )(*args)


def _out_ln_kernel(y_ref, w_ref, x_ref, g_ref, b_ref, o_ref, *, tm, rc):
    def body(ci, carry):
        rows = pl.ds(pl.multiple_of(ci * rc, rc), rc)
        f = jnp.dot(y_ref[rows, :], w_ref[...], preferred_element_type=F32)
        z = ALPHA * x_ref[rows, :] + f
        mu = jnp.mean(z, axis=1, keepdims=True)
        zc = z - mu
        var = jnp.mean(zc * zc, axis=1, keepdims=True)
        o_ref[rows, :] = zc * lax.rsqrt(var + LN_EPS) * g_ref[...] + b_ref[...]
        return carry

    lax.fori_loop(0, tm // rc, body, 0)


def _out_ln(y2d, w, x2d, g, b, *, tm=512, rc=256):
    n = x2d.shape[0]
    return pl.pallas_call(
        functools.partial(_out_ln_kernel, tm=tm, rc=rc),
        grid=(n // tm,),
        in_specs=[
            pl.BlockSpec((tm, D_MODEL), lambda i: (i, 0)),
            pl.BlockSpec((D_MODEL, D_MODEL), lambda i: (0, 0)),
            pl.BlockSpec((tm, D_MODEL), lambda i: (i, 0)),
            pl.BlockSpec((1, D_MODEL), lambda i: (0, 0)),
            pl.BlockSpec((1, D_MODEL), lambda i: (0, 0)),
        ],
        out_specs=pl.BlockSpec((tm, D_MODEL), lambda i: (i, 0)),
        out_shape=jax.ShapeDtypeStruct((n, D_MODEL), F32),
        compiler_params=_cparams(("arbitrary",)),
    )(y2d, w, x2d, g.reshape(1, D_MODEL), b.reshape(1, D_MODEL))


Q_SCALE_LOG2 = math.log2(math.e) / math.sqrt(HEAD_DIM)


def _scale_q(w):
    return jnp.concatenate([w[:, :D_MODEL] * Q_SCALE_LOG2, w[:, D_MODEL:]], axis=1).astype(BF16)


def _layer_a(x, p, layer_idx):
    batch, seq, _ = x.shape
    x2d = x.reshape(batch * seq, D_MODEL)
    w_in = _scale_q(p["w_in"])
    tabs = _rope_tables(seq)
    planes = _project(x2d, w_in, tabs, seq=seq, dil=1, tile=1024)
    lam_init = 0.8 - 0.6 * math.exp(-0.3 * layer_idx)
    lamv = jnp.stack([p["lam_q1"], p["lam_k1"], p["lam_q2"], p["lam_k2"]]).astype(F32)
    y = _attn_a(planes, lamv, p["subln_g"].astype(F32).reshape(1, LANES), batch=batch, seq=seq, lam_init=lam_init)
    out = _out_ln(y.reshape(batch * seq, D_MODEL), p["w_out"].astype(BF16), x2d, p["ln_g"], p["ln_b"])
    return out.reshape(batch, seq, D_MODEL)


def _layer_b(x, p):
    batch, seq, _ = x.shape
    x2d = x.reshape(batch * seq, D_MODEL)
    w_in = p["w_in"]
    tabs = _rope_tables(seq)
    group_planes = []
    gate_planes = None
    for gi, (_, dil) in enumerate(B_GROUPS):
        w = w_in[:, 3 * gi * D_MODEL:3 * (gi + 1) * D_MODEL]
        if dil == 1:
            w = jnp.concatenate([w, w_in[:, 3 * len(B_GROUPS) * D_MODEL:]], axis=1)
        planes = _project(x2d, _scale_q(w), tabs, seq=seq, dil=dil, tile=TILE)
        if dil == 1:
            gate_planes = planes
        group_planes.append(planes)
    y = _attn_b(group_planes, gate_planes, 3 * SECTION_PLANES, batch=batch, seq=seq)
    out = _out_ln(y.reshape(batch * seq, D_MODEL), p["w_out"].astype(BF16), x2d, p["ln_g"], p["ln_b"])
    return out.reshape(batch, seq, D_MODEL)


def _trunk(x, layers):
    for i, p in enumerate(layers):
        x = _layer_a(x, p, i) if i % 2 == 0 else _layer_b(x, p)
    return x


def kernel(x_prompt, x_sample, w_in_0, lam_q1_0, lam_k1_0, lam_q2_0, lam_k2_0, subln_g_0, w_out_0, ln_g_0, ln_b_0, w_in_1, w_out_1, ln_g_1, ln_b_1, w_in_2, lam_q1_2, lam_k1_2, lam_q2_2, lam_k2_2, subln_g_2, w_out_2, ln_g_2, ln_b_2, w_in_3, w_out_3, ln_g_3, ln_b_3):
    layers = [
        dict(w_in=w_in_0, lam_q1=lam_q1_0, lam_k1=lam_k1_0, lam_q2=lam_q2_0, lam_k2=lam_k2_0,
             subln_g=subln_g_0, w_out=w_out_0, ln_g=ln_g_0, ln_b=ln_b_0),
        dict(w_in=w_in_1, w_out=w_out_1, ln_g=ln_g_1, ln_b=ln_b_1),
        dict(w_in=w_in_2, lam_q1=lam_q1_2, lam_k1=lam_k1_2, lam_q2=lam_q2_2, lam_k2=lam_k2_2,
             subln_g=subln_g_2, w_out=w_out_2, ln_g=ln_g_2, ln_b=ln_b_2),
        dict(w_in=w_in_3, w_out=w_out_3, ln_g=ln_g_3, ln_b=ln_b_3),
    ]
    return (_trunk(x_prompt, layers), _trunk(x_sample, layers))
```

```python
import functools
import math

import jax
import jax.numpy as jnp
from jax import lax
from jax.experimental import pallas as pl
from jax.experimental.pallas import tpu as pltpu

F32 = jnp.float32
BF16 = jnp.bfloat16

D_MODEL = 1024
DEPTH = 4
ROPE_THETA = 500000.0
LN_EPS = 1e-5
ALPHA = (2.0 * DEPTH) ** 0.25
HEAD_DIM = 64
LANES = 128
N_PAIRS = D_MODEL // LANES
SECTION_PLANES = D_MODEL // LANES
B_GROUPS = ((128, 1), (512, 4), (2048, 16))
RADIUS = 64
TILE = 2048
HALO = 64
QBLK = 128
MASKED = -1e30
VMEM_LIMIT = 56 * 1024 * 1024


def _cparams(sem):
    return pltpu.CompilerParams(dimension_semantics=sem, vmem_limit_bytes=VMEM_LIMIT)


def _rope_tables(seq):
    half = HEAD_DIM // 8
    inv = ROPE_THETA ** (-jnp.arange(half, dtype=F32) / half)
    ang = jnp.arange(seq, dtype=F32)[:, None] * inv[None, :]
    cos, sin = jnp.cos(ang), jnp.sin(ang)
    ones = jnp.ones((seq, HEAD_DIM - 2 * half), F32)
    zeros8 = jnp.zeros((seq, half), F32)
    zeros_rest = jnp.zeros((seq, HEAD_DIM - 2 * half), F32)
    c = jnp.concatenate([cos, cos, ones], axis=1)
    sa = jnp.concatenate([-sin, zeros8, zeros_rest], axis=1)
    sb = jnp.concatenate([zeros8, sin, zeros_rest], axis=1)

    return tuple(jnp.concatenate([t, t], axis=1) for t in (c, sa, sb))


X_SLABS = D_MODEL // LANES


ROPE_COLS = 2 * D_MODEL


def _proj_kernel(*refs, dil, tm, sub_n, n_out):
    x_refs = refs[:X_SLABS]
    w_ref, c_ref, sa_ref, sb_ref, o_ref, xs_ref = refs[X_SLABS:]
    res_rows = tm // dil

    for c in range(X_SLABS):
        for r in range(dil):
            src = x_refs[c][pl.ds(r, res_rows, stride=dil), :] if dil > 1 else x_refs[c][...]
            xs_ref[r * res_rows:(r + 1) * res_rows, c * LANES:(c + 1) * LANES] = src.astype(BF16)

    for sd in range(n_out // sub_n):
        acc = jnp.dot(xs_ref[...], w_ref[:, sd * sub_n:(sd + 1) * sub_n], preferred_element_type=F32)
        for t in range(sub_n // LANES):
            a = acc[:, t * LANES:(t + 1) * LANES]
            if sd * sub_n < ROPE_COLS:
                a = a * c_ref[...] + pltpu.roll(a, LANES - 8, 1) * sa_ref[...] + pltpu.roll(a, 8, 1) * sb_ref[...]
            ab = a.astype(BF16)
            for r in range(dil):
                o_ref[sd * (sub_n // LANES) + t, r] = ab[r * res_rows:(r + 1) * res_rows]


def _project(x2d, w, tabs, *, seq, dil, tile, tm=512, sub_n=512):
    n = x2d.shape[0]
    n_out = w.shape[1]
    per_tile = tile // tm
    n_seq_tiles = seq // tm
    if dil > 1:
        tabs = tuple(t.reshape(seq // tm, tm // dil, dil, LANES).transpose(0, 2, 1, 3).reshape(seq, LANES)
                     for t in tabs)
    kern = functools.partial(_proj_kernel, dil=dil, tm=tm, sub_n=sub_n, n_out=n_out)
    tab_spec = pl.BlockSpec((tm, LANES), lambda i: (i % n_seq_tiles, 0))
    x_specs = [pl.BlockSpec((tm, LANES), lambda i, c=c: (i, c)) for c in range(X_SLABS)]
    return pl.pallas_call(
        kern,
        grid=(n // tm,),
        in_specs=x_specs + [
            pl.BlockSpec((D_MODEL, n_out), lambda i: (0, 0)),
            tab_spec, tab_spec, tab_spec,
        ],
        out_specs=pl.BlockSpec((n_out // LANES, None, dil, tm // dil, LANES),
                               lambda i: (0, i // per_tile, 0, i % per_tile, 0)),
        out_shape=jax.ShapeDtypeStruct((n_out // LANES, n // tile, dil, tile // dil, LANES), BF16),
        scratch_shapes=[pltpu.VMEM((tm, D_MODEL), BF16)],
        compiler_params=_cparams(("arbitrary",)),
    )(*([x2d] * X_SLABS), w, *tabs)


def _silu(g):
    return g * (1.0 / (1.0 + jnp.exp(-g)))


def _attn_a_kernel(lamv_ref, subg_ref, q_ref, k_ref, v_ref, g_ref, o_ref,
                   qs_ref, sa_ref, sb_ref, m_ref, l_ref, acc_ref, *, lam_init, bq, bk, nkv, rchunk):
    q = q_ref[...]
    lane = lax.broadcasted_iota(jnp.int32, q.shape, 1)
    zero = jnp.zeros_like(q)
    qs_ref[0:bq, :] = jnp.where(lane < HEAD_DIM, q, zero)
    qs_ref[bq:2 * bq, :] = jnp.where(lane >= HEAD_DIM, q, zero)
    m_ref[...] = jnp.full(m_ref.shape, -jnp.inf, F32)
    l_ref[...] = jnp.zeros(l_ref.shape, F32)
    acc_ref[...] = jnp.zeros(acc_ref.shape, F32)

    def scores(j, s_ref):
        kc = k_ref[pl.ds(pl.multiple_of(j * bk, bk), bk), :]
        s_ref[...] = lax.dot_general(qs_ref[...], kc, (((1,), (1,)), ((), ())), preferred_element_type=F32)

    def update(j, s_ref):
        vc = v_ref[pl.ds(pl.multiple_of(j * bk, bk), bk), :]
        for r0 in range(0, 2 * bq, rchunk):
            rows = slice(r0, r0 + rchunk)
            m_prev = m_ref[rows, :]
            m_new = jnp.maximum(m_prev, jnp.max(s_ref[rows, :], axis=1, keepdims=True))
            alpha = jnp.exp2(m_prev - m_new)
            p = jnp.exp2(s_ref[rows, :] - pltpu.repeat(m_new, bk // LANES, 1))
            psum = p[:, 0:LANES]
            for t in range(1, bk // LANES):
                psum = psum + p[:, t * LANES:(t + 1) * LANES]
            l_ref[rows, :] = alpha * l_ref[rows, :] + psum
            acc_ref[rows, :] = alpha * acc_ref[rows, :] + jnp.dot(p.astype(BF16), vc, preferred_element_type=F32)
            m_ref[rows, :] = m_new

    bufs = (sa_ref, sb_ref)
    scores(0, sa_ref)
    for j in range(nkv - 1):
        scores(j + 1, bufs[(j + 1) % 2])
        update(j, bufs[j % 2])
    update(nkv - 1, bufs[(nkv - 1) % 2])

    lv = lamv_ref[...]
    lam = (jnp.exp(jnp.sum(lv[0:1] * lv[1:2], axis=1, keepdims=True))
           - jnp.exp(jnp.sum(lv[2:3] * lv[3:4], axis=1, keepdims=True)) + lam_init)
    o = acc_ref[...] / jnp.sum(l_ref[...], axis=1, keepdims=True)
    od = o[0:bq] - lam * o[bq:2 * bq]
    od = od * lax.rsqrt(jnp.mean(od * od, axis=1, keepdims=True) + LN_EPS)
    od = od * subg_ref[...] * (1.0 - lam_init)
    o_ref[...] = (od * _silu(g_ref[...].astype(F32))).astype(BF16)


def _attn_a(planes, lamv, subg, *, batch, seq, lam_init, bq=512, bk=1024, rchunk=256):
    pl4 = planes.reshape(4 * SECTION_PLANES, batch, seq, LANES)
    nkv = seq // bk
    assert seq % bk == 0 and (2 * bq) % rchunk == 0
    kern = functools.partial(_attn_a_kernel, lam_init=lam_init, bq=bq, bk=bk, nkv=nkv, rchunk=rchunk)

    def q_rows(s):
        return pl.BlockSpec((None, None, bq, LANES), lambda b, h, qi: (s * SECTION_PLANES + h, b, qi, 0))

    def all_rows(s):
        return pl.BlockSpec((None, None, seq, LANES), lambda b, h, qi: (s * SECTION_PLANES + h, b, 0, 0))

    return pl.pallas_call(
        kern,
        grid=(batch, N_PAIRS, seq // bq),
        in_specs=[
            pl.BlockSpec((4, HEAD_DIM), lambda b, h, qi: (0, 0)),
            pl.BlockSpec((1, LANES), lambda b, h, qi: (0, 0)),
            q_rows(0), all_rows(1), all_rows(2), q_rows(3),
        ],
        out_specs=pl.BlockSpec((None, bq, LANES), lambda b, h, qi: (b, qi, h)),
        out_shape=jax.ShapeDtypeStruct((batch, seq, D_MODEL), BF16),
        scratch_shapes=[
            pltpu.VMEM((2 * bq, LANES), BF16),
            pltpu.VMEM((2 * bq, bk), F32),
            pltpu.VMEM((2 * bq, bk), F32),
            pltpu.VMEM((2 * bq, LANES), F32),
            pltpu.VMEM((2 * bq, LANES), F32),
            pltpu.VMEM((2 * bq, LANES), F32),
        ],
        compiler_params=_cparams(("arbitrary", "arbitrary", "arbitrary")),
    )(lamv, subg, pl4, pl4, pl4, pl4)


def _attn_b_group(gi, dil, seq, q_ref, k_ref, kp_ref, kn_ref, v_ref, vp_ref, vn_ref,
                  kw_ref, vw_ref, og_ref, mg_ref, lg_ref, bias_ref):
    rows = TILE // dil
    nb64 = rows // HALO
    tile_i = pl.program_id(2)
    length = seq // dil

    kw_ref[0:dil, 0:HALO, :] = kp_ref[...]
    vw_ref[0:dil, 0:HALO, :] = vp_ref[...]
    for n in range(nb64):
        kw_ref[0:dil, HALO + n * HALO:2 * HALO + n * HALO, :] = k_ref[:, n]
        vw_ref[0:dil, HALO + n * HALO:2 * HALO + n * HALO, :] = v_ref[:, n]
    kw_ref[0:dil, HALO + rows:2 * HALO + rows, :] = kn_ref[...]
    vw_ref[0:dil, HALO + rows:2 * HALO + rows, :] = vn_ref[...]

    win = QBLK + 2 * HALO
    col_i = lax.broadcasted_iota(jnp.int32, (1, win), 1)
    low_row = jnp.where(col_i < HALO, MASKED, 0.0)
    high_row = jnp.where(col_i >= QBLK + HALO, MASKED, 0.0)
    lane_q = lax.broadcasted_iota(jnp.int32, (QBLK, LANES), 1)
    lo_q = lane_q < HEAD_DIM
    blocks = rows // QBLK
    unroll = max(1, 8 // blocks)

    def block(r, blk):
        q2 = q_ref[r, 2 * blk:2 * blk + 2].reshape(QBLK, LANES)
        zero = jnp.zeros_like(q2)
        qs = jnp.concatenate([jnp.where(lo_q, q2, zero), jnp.where(lo_q, zero, q2)], axis=0)
        kwin = kw_ref[r, blk * QBLK:blk * QBLK + win, :]
        vwin = vw_ref[r, blk * QBLK:blk * QBLK + win, :]
        s = lax.dot_general(qs, kwin, (((1,), (1,)), ((), ())), preferred_element_type=F32)
        first = tile_i * rows + blk * QBLK
        edge = (jnp.where(first == 0, low_row, 0.0) + jnp.where(first + QBLK == length, high_row, 0.0))
        s = s + bias_ref[...] + edge
        m = jnp.max(s, axis=1, keepdims=True)
        p = jnp.exp2(s - m)
        l = jnp.sum(p, axis=1, keepdims=True)
        pb = p.astype(BF16)
        o0 = jnp.dot(pb[0:QBLK], vwin, preferred_element_type=F32)
        o1 = jnp.dot(pb[QBLK:2 * QBLK], vwin, preferred_element_type=F32)

        def per_head(a):
            return jnp.where(lo_q, jnp.broadcast_to(a[0:QBLK], (QBLK, LANES)),
                             jnp.broadcast_to(a[QBLK:2 * QBLK], (QBLK, LANES)))

        tok = pl.ds(r + blk * QBLK * dil, QBLK, stride=dil) if dil > 1 else pl.ds(blk * QBLK, QBLK)
        og_ref[gi, tok, :] = jnp.where(lo_q, o0, o1)
        mg_ref[gi, tok, :] = per_head(m)
        lg_ref[gi, tok, :] = per_head(l)

    def body(it, carry):
        for u in range(unroll):
            for blk in range(blocks):
                block(it * unroll + u, blk)
        return carry

    if dil == 1:
        body(0, 0)
    else:
        lax.fori_loop(0, dil // unroll, body, 0)


def _attn_b_kernel(*refs, seq):
    n_in = 7 * len(B_GROUPS) + 1
    gate_ref = refs[n_in - 1]
    o_ref = refs[n_in]
    kw_ref, vw_ref, og_ref, mg_ref, lg_ref, bias_ref = refs[n_in + 1:]

    win = QBLK + 2 * HALO
    row_i = lax.broadcasted_iota(jnp.int32, (2 * QBLK, win), 0)
    col_i = lax.broadcasted_iota(jnp.int32, (2 * QBLK, win), 1)
    qpos = jnp.where(row_i >= QBLK, row_i - QBLK, row_i)
    bias_ref[...] = jnp.where(jnp.abs(col_i - HALO - qpos) <= RADIUS, 0.0, MASKED)

    for gi, (_, dil) in enumerate(B_GROUPS):
        _attn_b_group(gi, dil, seq, *refs[7 * gi:7 * gi + 7], kw_ref, vw_ref, og_ref, mg_ref, lg_ref, bias_ref)
    m0, m1, m2 = mg_ref[0], mg_ref[1], mg_ref[2]
    mx = jnp.maximum(jnp.maximum(m0, m1), m2)
    e0, e1, e2 = jnp.exp2(m0 - mx), jnp.exp2(m1 - mx), jnp.exp2(m2 - mx)
    den = e0 * lg_ref[0] + e1 * lg_ref[1] + e2 * lg_ref[2]
    o = (e0 * og_ref[0] + e1 * og_ref[1] + e2 * og_ref[2]) / den
    o_ref[...] = (o * _silu(gate_ref[...].astype(F32))).astype(BF16)


def _attn_b(group_planes, gate_planes, gate_base, *, batch, seq):
    n_tiles = seq // TILE
    in_specs, args = [], []
    for (_, dil), planes in zip(B_GROUPS, group_planes):
        rows = TILE // dil
        nb64 = rows // HALO
        view = planes.reshape(planes.shape[0], batch, n_tiles, dil, nb64, HALO, LANES)

        def main(s):
            return pl.BlockSpec((None, None, None, dil, nb64, HALO, LANES),
                                lambda b, h, i, s=s: (s * SECTION_PLANES + h, b, i, 0, 0, 0, 0))

        def prev(s, nb64=nb64):
            return pl.BlockSpec((None, None, None, dil, None, HALO, LANES),
                                lambda b, h, i, s=s: (s * SECTION_PLANES + h, b, jnp.maximum(i - 1, 0), 0, nb64 - 1, 0, 0))

        def nxt(s):
            return pl.BlockSpec((None, None, None, dil, None, HALO, LANES),
                                lambda b, h, i, s=s: (s * SECTION_PLANES + h, b, jnp.minimum(i + 1, n_tiles - 1), 0, 0, 0, 0))

        in_specs += [main(0), main(1), prev(1), nxt(1), main(2), prev(2), nxt(2)]
        args += [view] * 7
    gate_view = gate_planes.reshape(gate_planes.shape[0], batch, seq, LANES)
    in_specs.append(pl.BlockSpec((None, None, TILE, LANES), lambda b, h, i: (gate_base + h, b, i, 0)))
    args.append(gate_view)
    max_dil = max(d for _, d in B_GROUPS)
    return pl.pallas_call(
        functools.partial(_attn_b_kernel, seq=seq),
        grid=(batch, N_PAIRS, n_tiles),
        in_specs=in_specs,
        out_specs=pl.BlockSpec((None, TILE, LANES), lambda b, h, i: (b, i, h)),
        out_shape=jax.ShapeDtypeStruct((batch, seq, D_MODEL), BF16),
        scratch_shapes=[
            pltpu.VMEM((max_dil, TILE + 2 * HALO, LANES), BF16),
            pltpu.VMEM((max_dil, TILE + 2 * HALO, LANES), BF16),
            pltpu.VMEM((len(B_GROUPS), TILE, LANES), F32),
            pltpu.VMEM((len(B_GROUPS), TILE, LANES), F32),
            pltpu.VMEM((len(B_GROUPS), TILE, LANES), F32),
            pltpu.VMEM((2 * QBLK, QBLK + 2 * HALO), F32),
        ],
        compiler_params=_cparams(("arbitrary", "arbitrary", "arbitrary")),
    )(*args)


def _out_ln_kernel(y_ref, w_ref, x_ref, g_ref, b_ref, o_ref, *, tm, rc):
    def body(ci, carry):
        rows = pl.ds(pl.multiple_of(ci * rc, rc), rc)
        f = jnp.dot(y_ref[rows, :], w_ref[...], preferred_element_type=F32)
        z = ALPHA * x_ref[rows, :] + f
        mu = jnp.mean(z, axis=1, keepdims=True)
        zc = z - mu
        var = jnp.mean(zc * zc, axis=1, keepdims=True)
        o_ref[rows, :] = zc * lax.rsqrt(var + LN_EPS) * g_ref[...] + b_ref[...]
        return carry

    lax.fori_loop(0, tm // rc, body, 0)


def _out_ln(y2d, w, x2d, g, b, *, tm=512, rc=256):
    n = x2d.shape[0]
    return pl.pallas_call(
        functools.partial(_out_ln_kernel, tm=tm, rc=rc),
        grid=(n // tm,),
        in_specs=[
            pl.BlockSpec((tm, D_MODEL), lambda i: (i, 0)),
            pl.BlockSpec((D_MODEL, D_MODEL), lambda i: (0, 0)),
            pl.BlockSpec((tm, D_MODEL), lambda i: (i, 0)),
            pl.BlockSpec((1, D_MODEL), lambda i: (0, 0)),
            pl.BlockSpec((1, D_MODEL), lambda i: (0, 0)),
        ],
        out_specs=pl.BlockSpec((tm, D_MODEL), lambda i: (i, 0)),
        out_shape=jax.ShapeDtypeStruct((n, D_MODEL), F32),
        compiler_params=_cparams(("arbitrary",)),
    )(y2d, w, x2d, g.reshape(1, D_MODEL), b.reshape(1, D_MODEL))


Q_SCALE_LOG2 = math.log2(math.e) / math.sqrt(HEAD_DIM)


def _scale_q(w):
    return jnp.concatenate([w[:, :D_MODEL] * Q_SCALE_LOG2, w[:, D_MODEL:]], axis=1).astype(BF16)


def _layer_a(x, p, layer_idx):
    batch, seq, _ = x.shape
    x2d = x.reshape(batch * seq, D_MODEL)
    w_in = _scale_q(p["w_in"])
    tabs = _rope_tables(seq)
    planes = _project(x2d, w_in, tabs, seq=seq, dil=1, tile=1024)
    lam_init = 0.8 - 0.6 * math.exp(-0.3 * layer_idx)
    lamv = jnp.stack([p["lam_q1"], p["lam_k1"], p["lam_q2"], p["lam_k2"]]).astype(F32)
    y = _attn_a(planes, lamv, p["subln_g"].astype(F32).reshape(1, LANES), batch=batch, seq=seq, lam_init=lam_init)
    out = _out_ln(y.reshape(batch * seq, D_MODEL), p["w_out"].astype(BF16), x2d, p["ln_g"], p["ln_b"])
    return out.reshape(batch, seq, D_MODEL)


def _layer_b(x, p):
    batch, seq, _ = x.shape
    x2d = x.reshape(batch * seq, D_MODEL)
    w_in = p["w_in"]
    tabs = _rope_tables(seq)
    group_planes = []
    gate_planes = None
    for gi, (_, dil) in enumerate(B_GROUPS):
        w = w_in[:, 3 * gi * D_MODEL:3 * (gi + 1) * D_MODEL]
        if dil == 1:
            w = jnp.concatenate([w, w_in[:, 3 * len(B_GROUPS) * D_MODEL:]], axis=1)
        planes = _project(x2d, _scale_q(w), tabs, seq=seq, dil=dil, tile=TILE)
        if dil == 1:
            gate_planes = planes
        group_planes.append(planes)
    y = _attn_b(group_planes, gate_planes, 3 * SECTION_PLANES, batch=batch, seq=seq)
    out = _out_ln(y.reshape(batch * seq, D_MODEL), p["w_out"].astype(BF16), x2d, p["ln_g"], p["ln_b"])
    return out.reshape(batch, seq, D_MODEL)


def _trunk(x, layers):
    for i, p in enumerate(layers):
        x = _layer_a(x, p, i) if i % 2 == 0 else _layer_b(x, p)
    return x


def kernel(x_prompt, x_sample, w_in_0, lam_q1_0, lam_k1_0, lam_q2_0, lam_k2_0, subln_g_0, w_out_0, ln_g_0, ln_b_0, w_in_1, w_out_1, ln_g_1, ln_b_1, w_in_2, lam_q1_2, lam_k1_2, lam_q2_2, lam_k2_2, subln_g_2, w_out_2, ln_g_2, ln_b_2, w_in_3, w_out_3, ln_g_3, ln_b_3):
    layers = [
        dict(w_in=w_in_0, lam_q1=lam_q1_0, lam_k1=lam_k1_0, lam_q2=lam_q2_0, lam_k2=lam_k2_0,
             subln_g=subln_g_0, w_out=w_out_0, ln_g=ln_g_0, ln_b=ln_b_0),
        dict(w_in=w_in_1, w_out=w_out_1, ln_g=ln_g_1, ln_b=ln_b_1),
        dict(w_in=w_in_2, lam_q1=lam_q1_2, lam_k1=lam_k1_2, lam_q2=lam_q2_2, lam_k2=lam_k2_2,
             subln_g=subln_g_2, w_out=w_out_2, ln_g=ln_g_2, ln_b=ln_b_2),
        dict(w_in=w_in_3, w_out=w_out_3, ln_g=ln_g_3, ln_b=ln_b_3),
    ]
    return (_trunk(x_prompt, layers), _trunk(x_sample, layers))
```

```python
import functools
import math

import jax
import jax.numpy as jnp
from jax import lax
from jax.experimental import pallas as pl
from jax.experimental.pallas import tpu as pltpu

F32 = jnp.float32
BF16 = jnp.bfloat16

D_MODEL = 1024
DEPTH = 4
ROPE_THETA = 500000.0
LN_EPS = 1e-5
ALPHA = (2.0 * DEPTH) ** 0.25
HEAD_DIM = 64
LANES = 128
N_PAIRS = D_MODEL // LANES
SECTION_PLANES = D_MODEL // LANES
B_GROUPS = ((128, 1), (512, 4), (2048, 16))
RADIUS = 64
TILE = 2048
HALO = 64
QBLK = 128
MASKED = -1e30
VMEM_LIMIT = 56 * 1024 * 1024


def _cparams(sem):
    return pltpu.CompilerParams(dimension_semantics=sem, vmem_limit_bytes=VMEM_LIMIT)


def _rope_tables(seq, dil, tm):
    half = HEAD_DIM // 8
    pos = jnp.arange(seq, dtype=jnp.int32).reshape(seq // tm, tm // dil, dil).transpose(0, 2, 1).reshape(seq)
    dim = jnp.arange(LANES, dtype=jnp.int32) % HEAD_DIM
    rotated = dim < 2 * half
    inv = jnp.where(rotated, ROPE_THETA ** (-(dim % half).astype(F32) / half), 0.0)
    ang = pos.astype(F32)[:, None] * inv[None, :]
    sin = jnp.sin(ang)
    c = jnp.cos(ang)
    sa = sin * jnp.where(dim < half, -1.0, 0.0)[None, :]
    sb = sin * jnp.where(rotated & (dim >= half), 1.0, 0.0)[None, :]
    return c, sa, sb


X_SLABS = D_MODEL // LANES


ROPE_COLS = 2 * D_MODEL


def _proj_kernel(*refs, dil, tm, sub_n, n_out):
    x_refs = refs[:X_SLABS]
    w_ref, c_ref, sa_ref, sb_ref, o_ref, xs_ref = refs[X_SLABS:]
    res_rows = tm // dil

    for c in range(X_SLABS):
        for r in range(dil):
            src = x_refs[c][pl.ds(r, res_rows, stride=dil), :] if dil > 1 else x_refs[c][...]
            xs_ref[r * res_rows:(r + 1) * res_rows, c * LANES:(c + 1) * LANES] = src.astype(BF16)

    for sd in range(n_out // sub_n):
        acc = jnp.dot(xs_ref[...], w_ref[:, sd * sub_n:(sd + 1) * sub_n], preferred_element_type=F32)
        for t in range(sub_n // LANES):
            a = acc[:, t * LANES:(t + 1) * LANES]
            if sd * sub_n < ROPE_COLS:
                a = a * c_ref[...] + pltpu.roll(a, LANES - 8, 1) * sa_ref[...] + pltpu.roll(a, 8, 1) * sb_ref[...]
            ab = a.astype(BF16)
            for r in range(dil):
                o_ref[sd * (sub_n // LANES) + t, r] = ab[r * res_rows:(r + 1) * res_rows]


def _project(x2d, w, *, seq, dil, tile, tm=512, sub_n=512):
    n = x2d.shape[0]
    n_out = w.shape[1]
    per_tile = tile // tm
    n_seq_tiles = seq // tm
    tabs = _rope_tables(seq, dil, tm)
    kern = functools.partial(_proj_kernel, dil=dil, tm=tm, sub_n=sub_n, n_out=n_out)
    tab_spec = pl.BlockSpec((tm, LANES), lambda i: (i % n_seq_tiles, 0))
    x_specs = [pl.BlockSpec((tm, LANES), lambda i, c=c: (i, c)) for c in range(X_SLABS)]
    return pl.pallas_call(
        kern,
        grid=(n // tm,),
        in_specs=x_specs + [
            pl.BlockSpec((D_MODEL, n_out), lambda i: (0, 0)),
            tab_spec, tab_spec, tab_spec,
        ],
        out_specs=pl.BlockSpec((n_out // LANES, None, dil, tm // dil, LANES),
                               lambda i: (0, i // per_tile, 0, i % per_tile, 0)),
        out_shape=jax.ShapeDtypeStruct((n_out // LANES, n // tile, dil, tile // dil, LANES), BF16),
        scratch_shapes=[pltpu.VMEM((tm, D_MODEL), BF16)],
        compiler_params=_cparams(("arbitrary",)),
    )(*([x2d] * X_SLABS), w, *tabs)


def _silu(g):
    return g * (1.0 / (1.0 + jnp.exp(-g)))


def _attn_a_kernel(lamv_ref, subg_ref, q_ref, k_ref, v_ref, g_ref, o_ref,
                   qt_ref, vt_ref, sa_ref, sb_ref, m_ref, l_ref, acc_ref, *, lam_init, bq, bk, nkv):
    @pl.when(pl.program_id(2) == 0)
    def _():
        for j in range(nkv):
            vt_ref[:, j * bk:(j + 1) * bk] = v_ref[j * bk:(j + 1) * bk, :].astype(F32).T.astype(BF16)

    qt = q_ref[...].astype(F32).T
    dim = lax.broadcasted_iota(jnp.int32, qt.shape, 0)
    qt_ref[:, 0:bq] = jnp.where(dim < HEAD_DIM, qt, 0.0).astype(BF16)
    qt_ref[:, bq:2 * bq] = jnp.where(dim >= HEAD_DIM, qt, 0.0).astype(BF16)
    m_ref[...] = jnp.full(m_ref.shape, -jnp.inf, F32)
    l_ref[...] = jnp.zeros(l_ref.shape, F32)
    acc_ref[...] = jnp.zeros(acc_ref.shape, F32)

    def scores(j, s_ref):
        kc = k_ref[pl.ds(pl.multiple_of(j * bk, bk), bk), :]
        s_ref[...] = jnp.dot(kc, qt_ref[...], preferred_element_type=F32)

    def update(j, s_ref):
        m_prev = m_ref[...]
        m_new = jnp.maximum(m_prev, jnp.max(s_ref[...], axis=0, keepdims=True))
        alpha = jnp.exp2(m_prev - m_new)
        p = jnp.exp2(s_ref[...] - m_new)
        l_ref[...] = alpha * l_ref[...] + jnp.sum(p, axis=0, keepdims=True)
        pv = jnp.dot(vt_ref[:, j * bk:(j + 1) * bk], p.astype(BF16), preferred_element_type=F32)
        acc_ref[...] = alpha * acc_ref[...] + pv
        m_ref[...] = m_new

    bufs = (sa_ref, sb_ref)
    scores(0, sa_ref)
    for j in range(nkv - 1):
        scores(j + 1, bufs[(j + 1) % 2])
        update(j, bufs[j % 2])
    update(nkv - 1, bufs[(nkv - 1) % 2])

    lv = lamv_ref[...]
    lam = (jnp.exp(jnp.sum(lv[0:1] * lv[1:2], axis=1, keepdims=True))
           - jnp.exp(jnp.sum(lv[2:3] * lv[3:4], axis=1, keepdims=True)) + lam_init)
    ot = acc_ref[...] / l_ref[...]
    odt = ot[:, 0:bq] - lam * ot[:, bq:2 * bq]
    odt = odt * lax.rsqrt(jnp.mean(odt * odt, axis=0, keepdims=True) + LN_EPS)
    od = odt.T * subg_ref[...] * (1.0 - lam_init)
    o_ref[...] = (od * _silu(g_ref[...].astype(F32))).astype(BF16)


def _attn_a(planes, lamv, subg, *, batch, seq, lam_init, bq=512, bk=1024):
    pl4 = planes.reshape(4 * SECTION_PLANES, batch, seq, LANES)
    nkv = seq // bk
    assert seq % bk == 0
    kern = functools.partial(_attn_a_kernel, lam_init=lam_init, bq=bq, bk=bk, nkv=nkv)

    def q_rows(s):
        return pl.BlockSpec((None, None, bq, LANES), lambda b, h, qi: (s * SECTION_PLANES + h, b, qi, 0))

    def all_rows(s):
        return pl.BlockSpec((None, None, seq, LANES), lambda b, h, qi: (s * SECTION_PLANES + h, b, 0, 0))

    return pl.pallas_call(
        kern,
        grid=(batch, N_PAIRS, seq // bq),
        in_specs=[
            pl.BlockSpec((4, HEAD_DIM), lambda b, h, qi: (0, 0)),
            pl.BlockSpec((1, LANES), lambda b, h, qi: (0, 0)),
            q_rows(0), all_rows(1), all_rows(2), q_rows(3),
        ],
        out_specs=pl.BlockSpec((None, bq, LANES), lambda b, h, qi: (b, qi, h)),
        out_shape=jax.ShapeDtypeStruct((batch, seq, D_MODEL), BF16),
        scratch_shapes=[
            pltpu.VMEM((LANES, 2 * bq), BF16),
            pltpu.VMEM((LANES, seq), BF16),
            pltpu.VMEM((bk, 2 * bq), F32),
            pltpu.VMEM((bk, 2 * bq), F32),
            pltpu.VMEM((1, 2 * bq), F32),
            pltpu.VMEM((1, 2 * bq), F32),
            pltpu.VMEM((LANES, 2 * bq), F32),
        ],
        compiler_params=_cparams(("arbitrary", "arbitrary", "arbitrary")),
    )(lamv, subg, pl4, pl4, pl4, pl4)


def _attn_b_group(gi, dil, seq, q_ref, k_ref, kp_ref, kn_ref, v_ref, vp_ref, vn_ref,
                  kw_ref, vw_ref, og_ref, mg_ref, lg_ref, bias_ref):
    rows = TILE // dil
    nb64 = rows // HALO
    tile_i = pl.program_id(2)
    length = seq // dil

    kw_ref[0:dil, 0:HALO, :] = kp_ref[...]
    vw_ref[0:dil, 0:HALO, :] = vp_ref[...]
    for n in range(nb64):
        kw_ref[0:dil, HALO + n * HALO:2 * HALO + n * HALO, :] = k_ref[:, n]
        vw_ref[0:dil, HALO + n * HALO:2 * HALO + n * HALO, :] = v_ref[:, n]
    kw_ref[0:dil, HALO + rows:2 * HALO + rows, :] = kn_ref[...]
    vw_ref[0:dil, HALO + rows:2 * HALO + rows, :] = vn_ref[...]

    win = QBLK + 2 * HALO
    col_i = lax.broadcasted_iota(jnp.int32, (1, win), 1)
    low_row = jnp.where(col_i < HALO, MASKED, 0.0)
    high_row = jnp.where(col_i >= QBLK + HALO, MASKED, 0.0)
    lane_q = lax.broadcasted_iota(jnp.int32, (QBLK, LANES), 1)
    lo_q = lane_q < HEAD_DIM
    blocks = rows // QBLK
    unroll = max(1, 8 // blocks)

    def block(r, blk):
        q2 = q_ref[r, 2 * blk:2 * blk + 2].reshape(QBLK, LANES)
        zero = jnp.zeros_like(q2)
        qs = jnp.concatenate([jnp.where(lo_q, q2, zero), jnp.where(lo_q, zero, q2)], axis=0)
        kwin = kw_ref[r, blk * QBLK:blk * QBLK + win, :]
        vwin = vw_ref[r, blk * QBLK:blk * QBLK + win, :]
        s = lax.dot_general(qs, kwin, (((1,), (1,)), ((), ())), preferred_element_type=F32)
        first = tile_i * rows + blk * QBLK
        edge = (jnp.where(first == 0, low_row, 0.0) + jnp.where(first + QBLK == length, high_row, 0.0))
        s = s + bias_ref[...] + edge
        m = jnp.max(s, axis=1, keepdims=True)
        p = jnp.exp2(s - m)
        l = jnp.sum(p, axis=1, keepdims=True)
        pb = p.astype(BF16)
        o0 = jnp.dot(pb[0:QBLK], vwin, preferred_element_type=F32)
        o1 = jnp.dot(pb[QBLK:2 * QBLK], vwin, preferred_element_type=F32)

        def per_head(a):
            return jnp.where(lo_q, jnp.broadcast_to(a[0:QBLK], (QBLK, LANES)),
                             jnp.broadcast_to(a[QBLK:2 * QBLK], (QBLK, LANES)))

        tok = pl.ds(r + blk * QBLK * dil, QBLK, stride=dil) if dil > 1 else pl.ds(blk * QBLK, QBLK)
        og_ref[gi, tok, :] = jnp.where(lo_q, o0, o1)
        mg_ref[gi, tok, :] = per_head(m)
        lg_ref[gi, tok, :] = per_head(l)

    def body(it, carry):
        for u in range(unroll):
            for blk in range(blocks):
                block(it * unroll + u, blk)
        return carry

    if dil == 1:
        body(0, 0)
    else:
        lax.fori_loop(0, dil // unroll, body, 0)


def _attn_b_kernel(*refs, seq):
    n_in = 7 * len(B_GROUPS) + 1
    gate_ref = refs[n_in - 1]
    o_ref = refs[n_in]
    kw_ref, vw_ref, og_ref, mg_ref, lg_ref, bias_ref = refs[n_in + 1:]

    win = QBLK + 2 * HALO
    row_i = lax.broadcasted_iota(jnp.int32, (2 * QBLK, win), 0)
    col_i = lax.broadcasted_iota(jnp.int32, (2 * QBLK, win), 1)
    qpos = jnp.where(row_i >= QBLK, row_i - QBLK, row_i)
    bias_ref[...] = jnp.where(jnp.abs(col_i - HALO - qpos) <= RADIUS, 0.0, MASKED)

    for gi, (_, dil) in enumerate(B_GROUPS):
        _attn_b_group(gi, dil, seq, *refs[7 * gi:7 * gi + 7], kw_ref, vw_ref, og_ref, mg_ref, lg_ref, bias_ref)
    m0, m1, m2 = mg_ref[0], mg_ref[1], mg_ref[2]
    mx = jnp.maximum(jnp.maximum(m0, m1), m2)
    e0, e1, e2 = jnp.exp2(m0 - mx), jnp.exp2(m1 - mx), jnp.exp2(m2 - mx)
    den = e0 * lg_ref[0] + e1 * lg_ref[1] + e2 * lg_ref[2]
    o = (e0 * og_ref[0] + e1 * og_ref[1] + e2 * og_ref[2]) / den
    o_ref[...] = (o * _silu(gate_ref[...].astype(F32))).astype(BF16)


def _attn_b(group_planes, gate_planes, gate_base, *, batch, seq):
    n_tiles = seq // TILE
    in_specs, args = [], []
    for (_, dil), planes in zip(B_GROUPS, group_planes):
        rows = TILE // dil
        nb64 = rows // HALO
        view = planes.reshape(planes.shape[0], batch, n_tiles, dil, nb64, HALO, LANES)

        def main(s):
            return pl.BlockSpec((None, None, None, dil, nb64, HALO, LANES),
                                lambda b, h, i, s=s: (s * SECTION_PLANES + h, b, i, 0, 0, 0, 0))

        def prev(s, nb64=nb64):
            return pl.BlockSpec((None, None, None, dil, None, HALO, LANES),
                                lambda b, h, i, s=s: (s * SECTION_PLANES + h, b, jnp.maximum(i - 1, 0), 0, nb64 - 1, 0, 0))

        def nxt(s):
            return pl.BlockSpec((None, None, None, dil, None, HALO, LANES),
                                lambda b, h, i, s=s: (s * SECTION_PLANES + h, b, jnp.minimum(i + 1, n_tiles - 1), 0, 0, 0, 0))

        in_specs += [main(0), main(1), prev(1), nxt(1), main(2), prev(2), nxt(2)]
        args += [view] * 7
    gate_view = gate_planes.reshape(gate_planes.shape[0], batch, seq, LANES)
    in_specs.append(pl.BlockSpec((None, None, TILE, LANES), lambda b, h, i: (gate_base + h, b, i, 0)))
    args.append(gate_view)
    max_dil = max(d for _, d in B_GROUPS)
    return pl.pallas_call(
        functools.partial(_attn_b_kernel, seq=seq),
        grid=(batch, N_PAIRS, n_tiles),
        in_specs=in_specs,
        out_specs=pl.BlockSpec((None, TILE, LANES), lambda b, h, i: (b, i, h)),
        out_shape=jax.ShapeDtypeStruct((batch, seq, D_MODEL), BF16),
        scratch_shapes=[
            pltpu.VMEM((max_dil, TILE + 2 * HALO, LANES), BF16),
            pltpu.VMEM((max_dil, TILE + 2 * HALO, LANES), BF16),
            pltpu.VMEM((len(B_GROUPS), TILE, LANES), F32),
            pltpu.VMEM((len(B_GROUPS), TILE, LANES), F32),
            pltpu.VMEM((len(B_GROUPS), TILE, LANES), F32),
            pltpu.VMEM((2 * QBLK, QBLK + 2 * HALO), F32),
        ],
        compiler_params=_cparams(("arbitrary", "arbitrary", "arbitrary")),
    )(*args)


def _out_ln_kernel(y_ref, w_ref, x_ref, g_ref, b_ref, o_ref, *, tm, rc):
    def body(ci, carry):
        rows = pl.ds(pl.multiple_of(ci * rc, rc), rc)
        f = jnp.dot(y_ref[rows, :], w_ref[...], preferred_element_type=F32)
        z = ALPHA * x_ref[rows, :] + f
        mu = jnp.mean(z, axis=1, keepdims=True)
        zc = z - mu
        var = jnp.mean(zc * zc, axis=1, keepdims=True)
        o_ref[rows, :] = zc * lax.rsqrt(var + LN_EPS) * g_ref[...] + b_ref[...]
        return carry

    lax.fori_loop(0, tm // rc, body, 0)


def _out_ln(y2d, w, x2d, g, b, *, tm=512, rc=256):
    n = x2d.shape[0]
    return pl.pallas_call(
        functools.partial(_out_ln_kernel, tm=tm, rc=rc),
        grid=(n // tm,),
        in_specs=[
            pl.BlockSpec((tm, D_MODEL), lambda i: (i, 0)),
            pl.BlockSpec((D_MODEL, D_MODEL), lambda i: (0, 0)),
            pl.BlockSpec((tm, D_MODEL), lambda i: (i, 0)),
            pl.BlockSpec((1, D_MODEL), lambda i: (0, 0)),
            pl.BlockSpec((1, D_MODEL), lambda i: (0, 0)),
        ],
        out_specs=pl.BlockSpec((tm, D_MODEL), lambda i: (i, 0)),
        out_shape=jax.ShapeDtypeStruct((n, D_MODEL), F32),
        compiler_params=_cparams(("arbitrary",)),
    )(y2d, w, x2d, g.reshape(1, D_MODEL), b.reshape(1, D_MODEL))


Q_SCALE_LOG2 = math.log2(math.e) / math.sqrt(HEAD_DIM)


def _scale_q(w):
    return jnp.concatenate([w[:, :D_MODEL] * Q_SCALE_LOG2, w[:, D_MODEL:]], axis=1).astype(BF16)


def _layer_a(x, p, layer_idx):
    batch, seq, _ = x.shape
    x2d = x.reshape(batch * seq, D_MODEL)
    w_in = _scale_q(p["w_in"])
    planes = _project(x2d, w_in, seq=seq, dil=1, tile=1024)
    lam_init = 0.8 - 0.6 * math.exp(-0.3 * layer_idx)
    lamv = jnp.stack([p["lam_q1"], p["lam_k1"], p["lam_q2"], p["lam_k2"]]).astype(F32)
    y = _attn_a(planes, lamv, p["subln_g"].astype(F32).reshape(1, LANES), batch=batch, seq=seq, lam_init=lam_init)
    out = _out_ln(y.reshape(batch * seq, D_MODEL), p["w_out"].astype(BF16), x2d, p["ln_g"], p["ln_b"])
    return out.reshape(batch, seq, D_MODEL)


def _layer_b(x, p):
    batch, seq, _ = x.shape
    x2d = x.reshape(batch * seq, D_MODEL)
    w_in = p["w_in"]
    group_planes = []
    gate_planes = None
    for gi, (_, dil) in enumerate(B_GROUPS):
        w = w_in[:, 3 * gi * D_MODEL:3 * (gi + 1) * D_MODEL]
        if dil == 1:
            w = jnp.concatenate([w, w_in[:, 3 * len(B_GROUPS) * D_MODEL:]], axis=1)
        planes = _project(x2d, _scale_q(w), seq=seq, dil=dil, tile=TILE)
        if dil == 1:
            gate_planes = planes
        group_planes.append(planes)
    y = _attn_b(group_planes, gate_planes, 3 * SECTION_PLANES, batch=batch, seq=seq)
    out = _out_ln(y.reshape(batch * seq, D_MODEL), p["w_out"].astype(BF16), x2d, p["ln_g"], p["ln_b"])
    return out.reshape(batch, seq, D_MODEL)


def _trunk(x, layers):
    for i, p in enumerate(layers):
        x = _layer_a(x, p, i) if i % 2 == 0 else _layer_b(x, p)
    return x


def kernel(x_prompt, x_sample, w_in_0, lam_q1_0, lam_k1_0, lam_q2_0, lam_k2_0, subln_g_0, w_out_0, ln_g_0, ln_b_0, w_in_1, w_out_1, ln_g_1, ln_b_1, w_in_2, lam_q1_2, lam_k1_2, lam_q2_2, lam_k2_2, subln_g_2, w_out_2, ln_g_2, ln_b_2, w_in_3, w_out_3, ln_g_3, ln_b_3):
    layers = [
        dict(w_in=w_in_0, lam_q1=lam_q1_0, lam_k1=lam_k1_0, lam_q2=lam_q2_0, lam_k2=lam_k2_0,
             subln_g=subln_g_0, w_out=w_out_0, ln_g=ln_g_0, ln_b=ln_b_0),
        dict(w_in=w_in_1, w_out=w_out_1, ln_g=ln_g_1, ln_b=ln_b_1),
        dict(w_in=w_in_2, lam_q1=lam_q1_2, lam_k1=lam_k1_2, lam_q2=lam_q2_2, lam_k2=lam_k2_2,
             subln_g=subln_g_2, w_out=w_out_2, ln_g=ln_g_2, ln_b=ln_b_2),
        dict(w_in=w_in_3, w_out=w_out_3, ln_g=ln_g_3, ln_b=ln_b_3),
    ]
    return (_trunk(x_prompt, layers), _trunk(x_sample, layers))
```

```python
import functools
import math

import jax
import jax.numpy as jnp
from jax import lax
from jax.experimental import pallas as pl
from jax.experimental.pallas import tpu as pltpu

F32 = jnp.float32
BF16 = jnp.bfloat16

D_MODEL = 1024
DEPTH = 4
ROPE_THETA = 500000.0
LN_EPS = 1e-5
ALPHA = (2.0 * DEPTH) ** 0.25
HEAD_DIM = 64
LANES = 128
N_PAIRS = D_MODEL // LANES
SECTION_PLANES = D_MODEL // LANES
B_GROUPS = ((128, 1), (512, 4), (2048, 16))
RADIUS = 64
TILE = 2048
HALO = 64
QBLK = 128
MASKED = -1e30
VMEM_LIMIT = 56 * 1024 * 1024


def _cparams(sem, flags=None):
    return pltpu.CompilerParams(dimension_semantics=sem, vmem_limit_bytes=VMEM_LIMIT, flags=flags)


def _rope_tables(seq, dil, tm):
    half = HEAD_DIM // 8
    pos = jnp.arange(seq, dtype=jnp.int32).reshape(seq // tm, tm // dil, dil).transpose(0, 2, 1).reshape(seq)
    dim = jnp.arange(LANES, dtype=jnp.int32) % HEAD_DIM
    rotated = dim < 2 * half
    inv = jnp.where(rotated, ROPE_THETA ** (-(dim % half).astype(F32) / half), 0.0)
    ang = pos.astype(F32)[:, None] * inv[None, :]
    sin = jnp.sin(ang)
    c = jnp.cos(ang)
    sa = sin * jnp.where(dim < half, -1.0, 0.0)[None, :]
    sb = sin * jnp.where(rotated & (dim >= half), 1.0, 0.0)[None, :]
    return c, sa, sb


X_SLABS = D_MODEL // LANES


ROPE_COLS = 2 * D_MODEL


def _proj_kernel(*refs, dil, tm, sub_n, n_out):
    x_refs = refs[:X_SLABS]
    w_ref, c_ref, sa_ref, sb_ref, o_ref, xs_ref = refs[X_SLABS:]
    res_rows = tm // dil

    for c in range(X_SLABS):
        for r in range(dil):
            src = x_refs[c][pl.ds(r, res_rows, stride=dil), :] if dil > 1 else x_refs[c][...]
            xs_ref[r * res_rows:(r + 1) * res_rows, c * LANES:(c + 1) * LANES] = src.astype(BF16)

    for sd in range(n_out // sub_n):
        acc = jnp.dot(xs_ref[...], w_ref[:, sd * sub_n:(sd + 1) * sub_n], preferred_element_type=F32)
        for t in range(sub_n // LANES):
            a = acc[:, t * LANES:(t + 1) * LANES]
            if sd * sub_n < ROPE_COLS:
                a = a * c_ref[...] + pltpu.roll(a, LANES - 8, 1) * sa_ref[...] + pltpu.roll(a, 8, 1) * sb_ref[...]
            ab = a.astype(BF16)
            for r in range(dil):
                o_ref[sd * (sub_n // LANES) + t, r] = ab[r * res_rows:(r + 1) * res_rows]


def _project(x2d, w, *, seq, dil, tile, tm=512, sub_n=512):
    n = x2d.shape[0]
    n_out = w.shape[1]
    per_tile = tile // tm
    n_seq_tiles = seq // tm
    tabs = _rope_tables(seq, dil, tm)
    kern = functools.partial(_proj_kernel, dil=dil, tm=tm, sub_n=sub_n, n_out=n_out)
    tab_spec = pl.BlockSpec((tm, LANES), lambda i: (i % n_seq_tiles, 0))
    x_specs = [pl.BlockSpec((tm, LANES), lambda i, c=c: (i, c)) for c in range(X_SLABS)]
    return pl.pallas_call(
        kern,
        grid=(n // tm,),
        in_specs=x_specs + [
            pl.BlockSpec((D_MODEL, n_out), lambda i: (0, 0)),
            tab_spec, tab_spec, tab_spec,
        ],
        out_specs=pl.BlockSpec((n_out // LANES, None, dil, tm // dil, LANES),
                               lambda i: (0, i // per_tile, 0, i % per_tile, 0)),
        out_shape=jax.ShapeDtypeStruct((n_out // LANES, n // tile, dil, tile // dil, LANES), BF16),
        scratch_shapes=[pltpu.VMEM((tm, D_MODEL), BF16)],
        compiler_params=_cparams(("arbitrary",)),
    )(*([x2d] * X_SLABS), w, *tabs)


ONES_ROWS = 16


def _silu(g):
    return g * (1.0 / (1.0 + jnp.exp(-g)))


def _attn_a_kernel(lamv_ref, subg_ref, q_ref, k_ref, v_ref, g_ref, o_ref,
                   qt_ref, vt_ref, sa_ref, sb_ref, m_ref, acc_ref, *, lam_init, bq, bk, nkv, qtile):
    @pl.when(pl.program_id(2) == 0)
    def _():
        for j in range(nkv):
            vt_ref[0:LANES, j * bk:(j + 1) * bk] = v_ref[j * bk:(j + 1) * bk, :].astype(F32).T.astype(BF16)
        vt_ref[LANES:LANES + ONES_ROWS, :] = jnp.ones((ONES_ROWS, vt_ref.shape[1]), BF16)

    qt = q_ref[...].astype(F32).T
    dim = lax.broadcasted_iota(jnp.int32, qt.shape, 0)
    qt_ref[:, 0:bq] = jnp.where(dim < HEAD_DIM, qt, 0.0).astype(BF16)
    qt_ref[:, bq:2 * bq] = jnp.where(dim >= HEAD_DIM, qt, 0.0).astype(BF16)
    m_ref[...] = jnp.full(m_ref.shape, -jnp.inf, F32)
    acc_ref[...] = jnp.zeros(acc_ref.shape, F32)

    def scores(j, s_ref):
        kc = k_ref[pl.ds(pl.multiple_of(j * bk, bk), bk), :]
        s_ref[...] = jnp.dot(kc, qt_ref[...], preferred_element_type=F32)

    def update(j, s_ref):
        for c0 in range(0, 2 * bq, qtile):
            cols = slice(c0, c0 + qtile)
            m_prev = m_ref[:, cols]
            m_new = jnp.maximum(m_prev, jnp.max(s_ref[:, cols], axis=0, keepdims=True))
            alpha = jnp.exp2(m_prev - m_new)
            p = jnp.exp2(s_ref[:, cols] - m_new)
            pv = jnp.dot(vt_ref[:, j * bk:(j + 1) * bk], p.astype(BF16), preferred_element_type=F32)
            acc_ref[:, cols] = alpha * acc_ref[:, cols] + pv
            m_ref[:, cols] = m_new

    bufs = (sa_ref, sb_ref)
    scores(0, sa_ref)
    for j in range(nkv - 1):
        scores(j + 1, bufs[(j + 1) % 2])
        update(j, bufs[j % 2])
    update(nkv - 1, bufs[(nkv - 1) % 2])

    lv = lamv_ref[...]
    lam = (jnp.exp(jnp.sum(lv[0:1] * lv[1:2], axis=1, keepdims=True))
           - jnp.exp(jnp.sum(lv[2:3] * lv[3:4], axis=1, keepdims=True)) + lam_init)
    ot = acc_ref[0:LANES, :] / acc_ref[LANES:LANES + 1, :]
    odt = ot[:, 0:bq] - lam * ot[:, bq:2 * bq]
    odt = odt * lax.rsqrt(jnp.mean(odt * odt, axis=0, keepdims=True) + LN_EPS)
    od = odt.T * subg_ref[...] * (1.0 - lam_init)
    o_ref[...] = (od * _silu(g_ref[...].astype(F32))).astype(BF16)


def _attn_a(planes, lamv, subg, *, batch, seq, lam_init, bq=512, bk=1024, qtile=1024):
    pl4 = planes.reshape(4 * SECTION_PLANES, batch, seq, LANES)
    nkv = seq // bk
    assert seq % bk == 0
    kern = functools.partial(_attn_a_kernel, lam_init=lam_init, bq=bq, bk=bk, nkv=nkv, qtile=qtile)

    def q_rows(s):
        return pl.BlockSpec((None, None, bq, LANES), lambda b, h, qi: (s * SECTION_PLANES + h, b, qi, 0))

    def all_rows(s):
        return pl.BlockSpec((None, None, seq, LANES), lambda b, h, qi: (s * SECTION_PLANES + h, b, 0, 0))

    return pl.pallas_call(
        kern,
        grid=(batch, N_PAIRS, seq // bq),
        in_specs=[
            pl.BlockSpec((4, HEAD_DIM), lambda b, h, qi: (0, 0)),
            pl.BlockSpec((1, LANES), lambda b, h, qi: (0, 0)),
            q_rows(0), all_rows(1), all_rows(2), q_rows(3),
        ],
        out_specs=pl.BlockSpec((None, bq, LANES), lambda b, h, qi: (b, qi, h)),
        out_shape=jax.ShapeDtypeStruct((batch, seq, D_MODEL), BF16),
        scratch_shapes=[
            pltpu.VMEM((LANES, 2 * bq), BF16),
            pltpu.VMEM((LANES + ONES_ROWS, seq), BF16),
            pltpu.VMEM((bk, 2 * bq), F32),
            pltpu.VMEM((bk, 2 * bq), F32),
            pltpu.VMEM((1, 2 * bq), F32),
            pltpu.VMEM((LANES + ONES_ROWS, 2 * bq), F32),
        ],
        compiler_params=_cparams(("arbitrary", "arbitrary", "arbitrary")),
    )(lamv, subg, pl4, pl4, pl4, pl4)


def _attn_b_group(gi, dil, seq, q_ref, k_ref, kp_ref, kn_ref, v_ref, vp_ref, vn_ref,
                  kw_ref, vw_ref, og_ref, mg_ref, lg_ref, bias_ref):
    rows = TILE // dil
    nb64 = rows // HALO
    tile_i = pl.program_id(2)
    length = seq // dil

    kw_ref[0:dil, 0:HALO, :] = kp_ref[...]
    vw_ref[0:dil, 0:HALO, :] = vp_ref[...]
    for n in range(nb64):
        kw_ref[0:dil, HALO + n * HALO:2 * HALO + n * HALO, :] = k_ref[:, n]
        vw_ref[0:dil, HALO + n * HALO:2 * HALO + n * HALO, :] = v_ref[:, n]
    kw_ref[0:dil, HALO + rows:2 * HALO + rows, :] = kn_ref[...]
    vw_ref[0:dil, HALO + rows:2 * HALO + rows, :] = vn_ref[...]

    win = QBLK + 2 * HALO
    col_i = lax.broadcasted_iota(jnp.int32, (1, win), 1)
    low_row = jnp.where(col_i < HALO, MASKED, 0.0)
    high_row = jnp.where(col_i >= QBLK + HALO, MASKED, 0.0)
    lane_q = lax.broadcasted_iota(jnp.int32, (QBLK, LANES), 1)
    lo_q = lane_q < HEAD_DIM
    blocks = rows // QBLK
    unroll = max(1, 8 // blocks)

    def block(r, blk):
        q2 = q_ref[r, 2 * blk:2 * blk + 2].reshape(QBLK, LANES)
        zero = jnp.zeros_like(q2)
        qs = jnp.concatenate([jnp.where(lo_q, q2, zero), jnp.where(lo_q, zero, q2)], axis=0)
        kwin = kw_ref[r, blk * QBLK:blk * QBLK + win, :]
        vwin = vw_ref[r, blk * QBLK:blk * QBLK + win, :]
        s = lax.dot_general(qs, kwin, (((1,), (1,)), ((), ())), preferred_element_type=F32)
        first = tile_i * rows + blk * QBLK
        edge = (jnp.where(first == 0, low_row, 0.0) + jnp.where(first + QBLK == length, high_row, 0.0))
        s = s + bias_ref[...] + edge
        m = jnp.max(s, axis=1, keepdims=True)
        p = jnp.exp2(s - m)
        l = jnp.sum(p, axis=1, keepdims=True)
        pb = p.astype(BF16)
        o0 = jnp.dot(pb[0:QBLK], vwin, preferred_element_type=F32)
        o1 = jnp.dot(pb[QBLK:2 * QBLK], vwin, preferred_element_type=F32)

        def per_head(a):
            return jnp.where(lo_q, jnp.broadcast_to(a[0:QBLK], (QBLK, LANES)),
                             jnp.broadcast_to(a[QBLK:2 * QBLK], (QBLK, LANES)))

        tok = pl.ds(r + blk * QBLK * dil, QBLK, stride=dil) if dil > 1 else pl.ds(blk * QBLK, QBLK)
        og_ref[gi, tok, :] = jnp.where(lo_q, o0, o1)
        mg_ref[gi, tok, :] = per_head(m)
        lg_ref[gi, tok, :] = per_head(l)

    def body(it, carry):
        for u in range(unroll):
            for blk in range(blocks):
                block(it * unroll + u, blk)
        return carry

    if dil == 1:
        body(0, 0)
    else:
        lax.fori_loop(0, dil // unroll, body, 0)


def _attn_b_kernel(*refs, seq):
    n_in = 7 * len(B_GROUPS) + 1
    gate_ref = refs[n_in - 1]
    o_ref = refs[n_in]
    kw_ref, vw_ref, og_ref, mg_ref, lg_ref, bias_ref = refs[n_in + 1:]

    win = QBLK + 2 * HALO
    row_i = lax.broadcasted_iota(jnp.int32, (2 * QBLK, win), 0)
    col_i = lax.broadcasted_iota(jnp.int32, (2 * QBLK, win), 1)
    qpos = jnp.where(row_i >= QBLK, row_i - QBLK, row_i)
    bias_ref[...] = jnp.where(jnp.abs(col_i - HALO - qpos) <= RADIUS, 0.0, MASKED)

    for gi, (_, dil) in enumerate(B_GROUPS):
        _attn_b_group(gi, dil, seq, *refs[7 * gi:7 * gi + 7], kw_ref, vw_ref, og_ref, mg_ref, lg_ref, bias_ref)
    m0, m1, m2 = mg_ref[0], mg_ref[1], mg_ref[2]
    mx = jnp.maximum(jnp.maximum(m0, m1), m2)
    e0, e1, e2 = jnp.exp2(m0 - mx), jnp.exp2(m1 - mx), jnp.exp2(m2 - mx)
    den = e0 * lg_ref[0] + e1 * lg_ref[1] + e2 * lg_ref[2]
    o = (e0 * og_ref[0] + e1 * og_ref[1] + e2 * og_ref[2]) / den
    o_ref[...] = (o * _silu(gate_ref[...].astype(F32))).astype(BF16)


def _attn_b(group_planes, gate_planes, gate_base, *, batch, seq):
    n_tiles = seq // TILE
    in_specs, args = [], []
    for (_, dil), planes in zip(B_GROUPS, group_planes):
        rows = TILE // dil
        nb64 = rows // HALO
        view = planes.reshape(planes.shape[0], batch, n_tiles, dil, nb64, HALO, LANES)

        def main(s):
            return pl.BlockSpec((None, None, None, dil, nb64, HALO, LANES),
                                lambda b, h, i, s=s: (s * SECTION_PLANES + h, b, i, 0, 0, 0, 0))

        def prev(s, nb64=nb64):
            return pl.BlockSpec((None, None, None, dil, None, HALO, LANES),
                                lambda b, h, i, s=s: (s * SECTION_PLANES + h, b, jnp.maximum(i - 1, 0), 0, nb64 - 1, 0, 0))

        def nxt(s):
            return pl.BlockSpec((None, None, None, dil, None, HALO, LANES),
                                lambda b, h, i, s=s: (s * SECTION_PLANES + h, b, jnp.minimum(i + 1, n_tiles - 1), 0, 0, 0, 0))

        in_specs += [main(0), main(1), prev(1), nxt(1), main(2), prev(2), nxt(2)]
        args += [view] * 7
    gate_view = gate_planes.reshape(gate_planes.shape[0], batch, seq, LANES)
    in_specs.append(pl.BlockSpec((None, None, TILE, LANES), lambda b, h, i: (gate_base + h, b, i, 0)))
    args.append(gate_view)
    max_dil = max(d for _, d in B_GROUPS)
    return pl.pallas_call(
        functools.partial(_attn_b_kernel, seq=seq),
        grid=(batch, N_PAIRS, n_tiles),
        in_specs=in_specs,
        out_specs=pl.BlockSpec((None, TILE, LANES), lambda b, h, i: (b, i, h)),
        out_shape=jax.ShapeDtypeStruct((batch, seq, D_MODEL), BF16),
        scratch_shapes=[
            pltpu.VMEM((max_dil, TILE + 2 * HALO, LANES), BF16),
            pltpu.VMEM((max_dil, TILE + 2 * HALO, LANES), BF16),
            pltpu.VMEM((len(B_GROUPS), TILE, LANES), F32),
            pltpu.VMEM((len(B_GROUPS), TILE, LANES), F32),
            pltpu.VMEM((len(B_GROUPS), TILE, LANES), F32),
            pltpu.VMEM((2 * QBLK, QBLK + 2 * HALO), F32),
        ],
        compiler_params=_cparams(("arbitrary", "arbitrary", "arbitrary")),
    )(*args)


def _out_ln_kernel(y_ref, w_ref, x_ref, g_ref, b_ref, o_ref, *, tm, rc):
    def body(ci, carry):
        rows = pl.ds(pl.multiple_of(ci * rc, rc), rc)
        f = jnp.dot(y_ref[rows, :], w_ref[...], preferred_element_type=F32)
        z = ALPHA * x_ref[rows, :] + f
        mu = jnp.mean(z, axis=1, keepdims=True)
        zc = z - mu
        var = jnp.mean(zc * zc, axis=1, keepdims=True)
        o_ref[rows, :] = zc * lax.rsqrt(var + LN_EPS) * g_ref[...] + b_ref[...]
        return carry

    lax.fori_loop(0, tm // rc, body, 0)


def _out_ln(y2d, w, x2d, g, b, *, tm=512, rc=256):
    n = x2d.shape[0]
    return pl.pallas_call(
        functools.partial(_out_ln_kernel, tm=tm, rc=rc),
        grid=(n // tm,),
        in_specs=[
            pl.BlockSpec((tm, D_MODEL), lambda i: (i, 0)),
            pl.BlockSpec((D_MODEL, D_MODEL), lambda i: (0, 0)),
            pl.BlockSpec((tm, D_MODEL), lambda i: (i, 0)),
            pl.BlockSpec((1, D_MODEL), lambda i: (0, 0)),
            pl.BlockSpec((1, D_MODEL), lambda i: (0, 0)),
        ],
        out_specs=pl.BlockSpec((tm, D_MODEL), lambda i: (i, 0)),
        out_shape=jax.ShapeDtypeStruct((n, D_MODEL), F32),
        compiler_params=_cparams(("arbitrary",)),
    )(y2d, w, x2d, g.reshape(1, D_MODEL), b.reshape(1, D_MODEL))


Q_SCALE_LOG2 = math.log2(math.e) / math.sqrt(HEAD_DIM)


def _scale_q(w):
    return jnp.concatenate([w[:, :D_MODEL] * Q_SCALE_LOG2, w[:, D_MODEL:]], axis=1).astype(BF16)


def _layer_a(x, p, layer_idx):
    batch, seq, _ = x.shape
    x2d = x.reshape(batch * seq, D_MODEL)
    w_in = _scale_q(p["w_in"])
    planes = _project(x2d, w_in, seq=seq, dil=1, tile=1024)
    lam_init = 0.8 - 0.6 * math.exp(-0.3 * layer_idx)
    lamv = jnp.stack([p["lam_q1"], p["lam_k1"], p["lam_q2"], p["lam_k2"]]).astype(F32)
    y = _attn_a(planes, lamv, p["subln_g"].astype(F32).reshape(1, LANES), batch=batch, seq=seq, lam_init=lam_init)
    out = _out_ln(y.reshape(batch * seq, D_MODEL), p["w_out"].astype(BF16), x2d, p["ln_g"], p["ln_b"])
    return out.reshape(batch, seq, D_MODEL)


def _layer_b(x, p):
    batch, seq, _ = x.shape
    x2d = x.reshape(batch * seq, D_MODEL)
    w_in = p["w_in"]
    group_planes = []
    gate_planes = None
    for gi, (_, dil) in enumerate(B_GROUPS):
        w = w_in[:, 3 * gi * D_MODEL:3 * (gi + 1) * D_MODEL]
        if dil == 1:
            w = jnp.concatenate([w, w_in[:, 3 * len(B_GROUPS) * D_MODEL:]], axis=1)
        planes = _project(x2d, _scale_q(w), seq=seq, dil=dil, tile=TILE)
        if dil == 1:
            gate_planes = planes
        group_planes.append(planes)
    y = _attn_b(group_planes, gate_planes, 3 * SECTION_PLANES, batch=batch, seq=seq)
    out = _out_ln(y.reshape(batch * seq, D_MODEL), p["w_out"].astype(BF16), x2d, p["ln_g"], p["ln_b"])
    return out.reshape(batch, seq, D_MODEL)


def _trunk(x, layers):
    for i, p in enumerate(layers):
        x = _layer_a(x, p, i) if i % 2 == 0 else _layer_b(x, p)
    return x


def kernel(x_prompt, x_sample, w_in_0, lam_q1_0, lam_k1_0, lam_q2_0, lam_k2_0, subln_g_0, w_out_0, ln_g_0, ln_b_0, w_in_1, w_out_1, ln_g_1, ln_b_1, w_in_2, lam_q1_2, lam_k1_2, lam_q2_2, lam_k2_2, subln_g_2, w_out_2, ln_g_2, ln_b_2, w_in_3, w_out_3, ln_g_3, ln_b_3):
    layers = [
        dict(w_in=w_in_0, lam_q1=lam_q1_0, lam_k1=lam_k1_0, lam_q2=lam_q2_0, lam_k2=lam_k2_0,
             subln_g=subln_g_0, w_out=w_out_0, ln_g=ln_g_0, ln_b=ln_b_0),
        dict(w_in=w_in_1, w_out=w_out_1, ln_g=ln_g_1, ln_b=ln_b_1),
        dict(w_in=w_in_2, lam_q1=lam_q1_2, lam_k1=lam_k1_2, lam_q2=lam_q2_2, lam_k2=lam_k2_2,
             subln_g=subln_g_2, w_out=w_out_2, ln_g=ln_g_2, ln_b=ln_b_2),
        dict(w_in=w_in_3, w_out=w_out_3, ln_g=ln_g_3, ln_b=ln_b_3),
    ]
    return (_trunk(x_prompt, layers), _trunk(x_sample, layers))
```

```python
import functools
import math

import jax
import jax.numpy as jnp
from jax import lax
from jax.experimental import pallas as pl
from jax.experimental.pallas import tpu as pltpu

F32 = jnp.float32
BF16 = jnp.bfloat16

D_MODEL = 1024
DEPTH = 4
ROPE_THETA = 500000.0
LN_EPS = 1e-5
ALPHA = (2.0 * DEPTH) ** 0.25
HEAD_DIM = 64
LANES = 128
N_PAIRS = D_MODEL // LANES
SECTION_PLANES = D_MODEL // LANES
B_GROUPS = ((128, 1), (512, 4), (2048, 16))
RADIUS = 64
TILE = 2048
HALO = 64
QBLK = 128
MASKED = -1e30
VMEM_LIMIT = 56 * 1024 * 1024


def _cparams(sem, flags=None):
    return pltpu.CompilerParams(dimension_semantics=sem, vmem_limit_bytes=VMEM_LIMIT, flags=flags)


def _rope_tables(seq, dil, tm):
    half = HEAD_DIM // 8
    pos = jnp.arange(seq, dtype=jnp.int32).reshape(seq // tm, tm // dil, dil).transpose(0, 2, 1).reshape(seq)
    dim = jnp.arange(LANES, dtype=jnp.int32) % HEAD_DIM
    rotated = dim < 2 * half
    inv = jnp.where(rotated, ROPE_THETA ** (-(dim % half).astype(F32) / half), 0.0)
    ang = pos.astype(F32)[:, None] * inv[None, :]
    sin = jnp.sin(ang)
    c = jnp.cos(ang)
    sa = sin * jnp.where(dim < half, -1.0, 0.0)[None, :]
    sb = sin * jnp.where(rotated & (dim >= half), 1.0, 0.0)[None, :]
    return c, sa, sb


X_SLABS = D_MODEL // LANES


ROPE_COLS = 2 * D_MODEL


def _proj_kernel(*refs, dil, tm, sub_n, n_out):
    x_refs = refs[:X_SLABS]
    w_ref, c_ref, sa_ref, sb_ref, o_ref, xs_ref = refs[X_SLABS:]
    res_rows = tm // dil

    for c in range(X_SLABS):
        for r in range(dil):
            src = x_refs[c][pl.ds(r, res_rows, stride=dil), :] if dil > 1 else x_refs[c][...]
            xs_ref[r * res_rows:(r + 1) * res_rows, c * LANES:(c + 1) * LANES] = src.astype(BF16)

    for sd in range(n_out // sub_n):
        acc = jnp.dot(xs_ref[...], w_ref[:, sd * sub_n:(sd + 1) * sub_n], preferred_element_type=F32)
        for t in range(sub_n // LANES):
            a = acc[:, t * LANES:(t + 1) * LANES]
            if sd * sub_n < ROPE_COLS:
                a = a * c_ref[...] + pltpu.roll(a, LANES - 8, 1) * sa_ref[...] + pltpu.roll(a, 8, 1) * sb_ref[...]
            ab = a.astype(BF16)
            for r in range(dil):
                o_ref[sd * (sub_n // LANES) + t, r] = ab[r * res_rows:(r + 1) * res_rows]


def _project(x2d, w, *, seq, dil, tile, tm=512, sub_n=512):
    n = x2d.shape[0]
    n_out = w.shape[1]
    per_tile = tile // tm
    n_seq_tiles = seq // tm
    tabs = _rope_tables(seq, dil, tm)
    kern = functools.partial(_proj_kernel, dil=dil, tm=tm, sub_n=sub_n, n_out=n_out)
    tab_spec = pl.BlockSpec((tm, LANES), lambda i: (i % n_seq_tiles, 0))
    x_specs = [pl.BlockSpec((tm, LANES), lambda i, c=c: (i, c)) for c in range(X_SLABS)]
    return pl.pallas_call(
        kern,
        grid=(n // tm,),
        in_specs=x_specs + [
            pl.BlockSpec((D_MODEL, n_out), lambda i: (0, 0)),
            tab_spec, tab_spec, tab_spec,
        ],
        out_specs=pl.BlockSpec((n_out // LANES, None, dil, tm // dil, LANES),
                               lambda i: (0, i // per_tile, 0, i % per_tile, 0)),
        out_shape=jax.ShapeDtypeStruct((n_out // LANES, n // tile, dil, tile // dil, LANES), BF16),
        scratch_shapes=[pltpu.VMEM((tm, D_MODEL), BF16)],
        compiler_params=_cparams(("arbitrary",)),
    )(*([x2d] * X_SLABS), w, *tabs)


N_SCORE_BUFS = 2
ONES_ROWS = 16


def _silu(g):
    return g * (1.0 / (1.0 + jnp.exp(-g)))


def _attn_a_kernel(lamv_ref, subg_ref, q_ref, k_ref, v_ref, g_ref, o_ref,
                   qt_ref, vt_ref, s0_ref, s1_ref, m_ref, acc_ref, *, lam_init, bq, bk, nkv, qtile):
    @pl.when(pl.program_id(2) == 0)
    def _():
        for j in range(nkv):
            vt_ref[0:LANES, j * bk:(j + 1) * bk] = v_ref[j * bk:(j + 1) * bk, :].astype(F32).T.astype(BF16)
        vt_ref[LANES:LANES + ONES_ROWS, :] = jnp.ones((ONES_ROWS, vt_ref.shape[1]), BF16)

    qt = q_ref[...].astype(F32).T
    dim = lax.broadcasted_iota(jnp.int32, qt.shape, 0)
    qt_ref[:, 0:bq] = jnp.where(dim < HEAD_DIM, qt, 0.0).astype(BF16)
    qt_ref[:, bq:2 * bq] = jnp.where(dim >= HEAD_DIM, qt, 0.0).astype(BF16)
    m_ref[...] = jnp.full(m_ref.shape, -jnp.inf, F32)
    acc_ref[...] = jnp.zeros(acc_ref.shape, F32)

    tiles = [slice(c0, c0 + qtile) for c0 in range(0, 2 * bq, qtile)]

    def scores(j, s_ref, cols):
        kc = k_ref[pl.ds(pl.multiple_of(j * bk, bk), bk), :]
        s_ref[:, cols] = jnp.dot(kc, qt_ref[:, cols], preferred_element_type=F32).astype(BF16)

    def update(j, s_ref, cols):
        m_prev = m_ref[:, cols]
        m_new = jnp.maximum(m_prev, jnp.max(s_ref[:, cols], axis=0, keepdims=True).astype(F32))
        alpha = jnp.exp2(m_prev - m_new)
        p = jnp.exp2(s_ref[:, cols] - m_new.astype(BF16))
        pv = jnp.dot(vt_ref[:, j * bk:(j + 1) * bk], p, preferred_element_type=F32)
        acc_ref[:, cols] = alpha * acc_ref[:, cols] + pv
        m_ref[:, cols] = m_new

    bufs = (s0_ref, s1_ref)
    for cols in tiles:
        scores(0, bufs[0], cols)
    for j in range(nkv - 1):
        for cols in tiles:
            scores(j + 1, bufs[(j + 1) % N_SCORE_BUFS], cols)
            update(j, bufs[j % N_SCORE_BUFS], cols)
    for cols in tiles:
        update(nkv - 1, bufs[(nkv - 1) % N_SCORE_BUFS], cols)

    lv = lamv_ref[...]
    lam = (jnp.exp(jnp.sum(lv[0:1] * lv[1:2], axis=1, keepdims=True))
           - jnp.exp(jnp.sum(lv[2:3] * lv[3:4], axis=1, keepdims=True)) + lam_init)
    ot = acc_ref[0:LANES, :] / acc_ref[LANES:LANES + 1, :]
    odt = ot[:, 0:bq] - lam * ot[:, bq:2 * bq]
    odt = odt * lax.rsqrt(jnp.mean(odt * odt, axis=0, keepdims=True) + LN_EPS)
    od = odt.T * subg_ref[...] * (1.0 - lam_init)
    o_ref[...] = (od * _silu(g_ref[...].astype(F32))).astype(BF16)


def _attn_a(planes, lamv, subg, *, batch, seq, lam_init, bq=512, bk=1024, qtile=256):
    pl4 = planes.reshape(4 * SECTION_PLANES, batch, seq, LANES)
    nkv = seq // bk
    assert seq % bk == 0 and (2 * bq) % qtile == 0
    kern = functools.partial(_attn_a_kernel, lam_init=lam_init, bq=bq, bk=bk, nkv=nkv, qtile=qtile)

    def q_rows(s):
        return pl.BlockSpec((None, None, bq, LANES), lambda b, h, qi: (s * SECTION_PLANES + h, b, qi, 0))

    def all_rows(s):
        return pl.BlockSpec((None, None, seq, LANES), lambda b, h, qi: (s * SECTION_PLANES + h, b, 0, 0))

    return pl.pallas_call(
        kern,
        grid=(batch, N_PAIRS, seq // bq),
        in_specs=[
            pl.BlockSpec((4, HEAD_DIM), lambda b, h, qi: (0, 0)),
            pl.BlockSpec((1, LANES), lambda b, h, qi: (0, 0)),
            q_rows(0), all_rows(1), all_rows(2), q_rows(3),
        ],
        out_specs=pl.BlockSpec((None, bq, LANES), lambda b, h, qi: (b, qi, h)),
        out_shape=jax.ShapeDtypeStruct((batch, seq, D_MODEL), BF16),
        scratch_shapes=[
            pltpu.VMEM((LANES, 2 * bq), BF16),
            pltpu.VMEM((LANES + ONES_ROWS, seq), BF16),
        ] + [pltpu.VMEM((bk, 2 * bq), BF16)] * N_SCORE_BUFS + [
            pltpu.VMEM((1, 2 * bq), F32),
            pltpu.VMEM((LANES + ONES_ROWS, 2 * bq), F32),
        ],
        compiler_params=_cparams(("arbitrary", "arbitrary", "arbitrary")),
    )(lamv, subg, pl4, pl4, pl4, pl4)


def _attn_b_group(gi, dil, seq, q_ref, k_ref, kp_ref, kn_ref, v_ref, vp_ref, vn_ref,
                  kw_ref, vw_ref, og_ref, mg_ref, lg_ref, bias_ref):
    rows = TILE // dil
    nb64 = rows // HALO
    tile_i = pl.program_id(2)
    length = seq // dil

    kw_ref[0:dil, 0:HALO, :] = kp_ref[...]
    vw_ref[0:dil, 0:HALO, :] = vp_ref[...]
    for n in range(nb64):
        kw_ref[0:dil, HALO + n * HALO:2 * HALO + n * HALO, :] = k_ref[:, n]
        vw_ref[0:dil, HALO + n * HALO:2 * HALO + n * HALO, :] = v_ref[:, n]
    kw_ref[0:dil, HALO + rows:2 * HALO + rows, :] = kn_ref[...]
    vw_ref[0:dil, HALO + rows:2 * HALO + rows, :] = vn_ref[...]

    win = QBLK + 2 * HALO
    col_i = lax.broadcasted_iota(jnp.int32, (1, win), 1)
    low_row = jnp.where(col_i < HALO, MASKED, 0.0)
    high_row = jnp.where(col_i >= QBLK + HALO, MASKED, 0.0)
    lane_q = lax.broadcasted_iota(jnp.int32, (QBLK, LANES), 1)
    lo_q = lane_q < HEAD_DIM
    blocks = rows // QBLK
    unroll = max(1, 8 // blocks)

    def block(r, blk):
        q2 = q_ref[r, 2 * blk:2 * blk + 2].reshape(QBLK, LANES)
        zero = jnp.zeros_like(q2)
        qs = jnp.concatenate([jnp.where(lo_q, q2, zero), jnp.where(lo_q, zero, q2)], axis=0)
        kwin = kw_ref[r, blk * QBLK:blk * QBLK + win, :]
        vwin = vw_ref[r, blk * QBLK:blk * QBLK + win, :]
        s = lax.dot_general(qs, kwin, (((1,), (1,)), ((), ())), preferred_element_type=F32)
        first = tile_i * rows + blk * QBLK
        edge = (jnp.where(first == 0, low_row, 0.0) + jnp.where(first + QBLK == length, high_row, 0.0))
        s = s + bias_ref[...] + edge
        m = jnp.max(s, axis=1, keepdims=True)
        p = jnp.exp2(s - m)
        l = jnp.sum(p, axis=1, keepdims=True)
        pb = p.astype(BF16)
        o0 = jnp.dot(pb[0:QBLK], vwin, preferred_element_type=F32)
        o1 = jnp.dot(pb[QBLK:2 * QBLK], vwin, preferred_element_type=F32)

        def per_head(a):
            return jnp.where(lo_q, jnp.broadcast_to(a[0:QBLK], (QBLK, LANES)),
                             jnp.broadcast_to(a[QBLK:2 * QBLK], (QBLK, LANES)))

        tok = pl.ds(r + blk * QBLK * dil, QBLK, stride=dil) if dil > 1 else pl.ds(blk * QBLK, QBLK)
        og_ref[gi, tok, :] = jnp.where(lo_q, o0, o1)
        mg_ref[gi, tok, :] = per_head(m)
        lg_ref[gi, tok, :] = per_head(l)

    def body(it, carry):
        for u in range(unroll):
            for blk in range(blocks):
                block(it * unroll + u, blk)
        return carry

    if dil == 1:
        body(0, 0)
    else:
        lax.fori_loop(0, dil // unroll, body, 0)


def _attn_b_kernel(*refs, seq):
    n_in = 7 * len(B_GROUPS) + 1
    gate_ref = refs[n_in - 1]
    o_ref = refs[n_in]
    kw_ref, vw_ref, og_ref, mg_ref, lg_ref, bias_ref = refs[n_in + 1:]

    win = QBLK + 2 * HALO
    row_i = lax.broadcasted_iota(jnp.int32, (2 * QBLK, win), 0)
    col_i = lax.broadcasted_iota(jnp.int32, (2 * QBLK, win), 1)
    qpos = jnp.where(row_i >= QBLK, row_i - QBLK, row_i)
    bias_ref[...] = jnp.where(jnp.abs(col_i - HALO - qpos) <= RADIUS, 0.0, MASKED)

    for gi, (_, dil) in enumerate(B_GROUPS):
        _attn_b_group(gi, dil, seq, *refs[7 * gi:7 * gi + 7], kw_ref, vw_ref, og_ref, mg_ref, lg_ref, bias_ref)
    m0, m1, m2 = mg_ref[0], mg_ref[1], mg_ref[2]
    mx = jnp.maximum(jnp.maximum(m0, m1), m2)
    e0, e1, e2 = jnp.exp2(m0 - mx), jnp.exp2(m1 - mx), jnp.exp2(m2 - mx)
    den = e0 * lg_ref[0] + e1 * lg_ref[1] + e2 * lg_ref[2]
    o = (e0 * og_ref[0] + e1 * og_ref[1] + e2 * og_ref[2]) / den
    o_ref[...] = (o * _silu(gate_ref[...].astype(F32))).astype(BF16)


def _attn_b(group_planes, gate_planes, gate_base, *, batch, seq):
    n_tiles = seq // TILE
    in_specs, args = [], []
    for (_, dil), planes in zip(B_GROUPS, group_planes):
        rows = TILE // dil
        nb64 = rows // HALO
        view = planes.reshape(planes.shape[0], batch, n_tiles, dil, nb64, HALO, LANES)

        def main(s):
            return pl.BlockSpec((None, None, None, dil, nb64, HALO, LANES),
                                lambda b, h, i, s=s: (s * SECTION_PLANES + h, b, i, 0, 0, 0, 0))

        def prev(s, nb64=nb64):
            return pl.BlockSpec((None, None, None, dil, None, HALO, LANES),
                                lambda b, h, i, s=s: (s * SECTION_PLANES + h, b, jnp.maximum(i - 1, 0), 0, nb64 - 1, 0, 0))

        def nxt(s):
            return pl.BlockSpec((None, None, None, dil, None, HALO, LANES),
                                lambda b, h, i, s=s: (s * SECTION_PLANES + h, b, jnp.minimum(i + 1, n_tiles - 1), 0, 0, 0, 0))

        in_specs += [main(0), main(1), prev(1), nxt(1), main(2), prev(2), nxt(2)]
        args += [view] * 7
    gate_view = gate_planes.reshape(gate_planes.shape[0], batch, seq, LANES)
    in_specs.append(pl.BlockSpec((None, None, TILE, LANES), lambda b, h, i: (gate_base + h, b, i, 0)))
    args.append(gate_view)
    max_dil = max(d for _, d in B_GROUPS)
    return pl.pallas_call(
        functools.partial(_attn_b_kernel, seq=seq),
        grid=(batch, N_PAIRS, n_tiles),
        in_specs=in_specs,
        out_specs=pl.BlockSpec((None, TILE, LANES), lambda b, h, i: (b, i, h)),
        out_shape=jax.ShapeDtypeStruct((batch, seq, D_MODEL), BF16),
        scratch_shapes=[
            pltpu.VMEM((max_dil, TILE + 2 * HALO, LANES), BF16),
            pltpu.VMEM((max_dil, TILE + 2 * HALO, LANES), BF16),
            pltpu.VMEM((len(B_GROUPS), TILE, LANES), F32),
            pltpu.VMEM((len(B_GROUPS), TILE, LANES), F32),
            pltpu.VMEM((len(B_GROUPS), TILE, LANES), F32),
            pltpu.VMEM((2 * QBLK, QBLK + 2 * HALO), F32),
        ],
        compiler_params=_cparams(("arbitrary", "arbitrary", "arbitrary")),
    )(*args)


def _out_ln_kernel(y_ref, w_ref, x_ref, g_ref, b_ref, o_ref, *, tm, rc):
    def body(ci, carry):
        rows = pl.ds(pl.multiple_of(ci * rc, rc), rc)
        f = jnp.dot(y_ref[rows, :], w_ref[...], preferred_element_type=F32)
        z = ALPHA * x_ref[rows, :] + f
        mu = jnp.mean(z, axis=1, keepdims=True)
        zc = z - mu
        var = jnp.mean(zc * zc, axis=1, keepdims=True)
        o_ref[rows, :] = zc * lax.rsqrt(var + LN_EPS) * g_ref[...] + b_ref[...]
        return carry

    lax.fori_loop(0, tm // rc, body, 0)


def _out_ln(y2d, w, x2d, g, b, *, tm=512, rc=256):
    n = x2d.shape[0]
    return pl.pallas_call(
        functools.partial(_out_ln_kernel, tm=tm, rc=rc),
        grid=(n // tm,),
        in_specs=[
            pl.BlockSpec((tm, D_MODEL), lambda i: (i, 0)),
            pl.BlockSpec((D_MODEL, D_MODEL), lambda i: (0, 0)),
            pl.BlockSpec((tm, D_MODEL), lambda i: (i, 0)),
            pl.BlockSpec((1, D_MODEL), lambda i: (0, 0)),
            pl.BlockSpec((1, D_MODEL), lambda i: (0, 0)),
        ],
        out_specs=pl.BlockSpec((tm, D_MODEL), lambda i: (i, 0)),
        out_shape=jax.ShapeDtypeStruct((n, D_MODEL), F32),
        compiler_params=_cparams(("arbitrary",)),
    )(y2d, w, x2d, g.reshape(1, D_MODEL), b.reshape(1, D_MODEL))


Q_SCALE_LOG2 = math.log2(math.e) / math.sqrt(HEAD_DIM)


def _scale_q(w):
    return jnp.concatenate([w[:, :D_MODEL] * Q_SCALE_LOG2, w[:, D_MODEL:]], axis=1).astype(BF16)


def _layer_a(x, p, layer_idx):
    batch, seq, _ = x.shape
    x2d = x.reshape(batch * seq, D_MODEL)
    w_in = _scale_q(p["w_in"])
    planes = _project(x2d, w_in, seq=seq, dil=1, tile=1024)
    lam_init = 0.8 - 0.6 * math.exp(-0.3 * layer_idx)
    lamv = jnp.stack([p["lam_q1"], p["lam_k1"], p["lam_q2"], p["lam_k2"]]).astype(F32)
    y = _attn_a(planes, lamv, p["subln_g"].astype(F32).reshape(1, LANES), batch=batch, seq=seq, lam_init=lam_init)
    out = _out_ln(y.reshape(batch * seq, D_MODEL), p["w_out"].astype(BF16), x2d, p["ln_g"], p["ln_b"])
    return out.reshape(batch, seq, D_MODEL)


def _layer_b(x, p):
    batch, seq, _ = x.shape
    x2d = x.reshape(batch * seq, D_MODEL)
    w_in = p["w_in"]
    group_planes = []
    gate_planes = None
    for gi, (_, dil) in enumerate(B_GROUPS):
        w = w_in[:, 3 * gi * D_MODEL:3 * (gi + 1) * D_MODEL]
        if dil == 1:
            w = jnp.concatenate([w, w_in[:, 3 * len(B_GROUPS) * D_MODEL:]], axis=1)
        planes = _project(x2d, _scale_q(w), seq=seq, dil=dil, tile=TILE)
        if dil == 1:
            gate_planes = planes
        group_planes.append(planes)
    y = _attn_b(group_planes, gate_planes, 3 * SECTION_PLANES, batch=batch, seq=seq)
    out = _out_ln(y.reshape(batch * seq, D_MODEL), p["w_out"].astype(BF16), x2d, p["ln_g"], p["ln_b"])
    return out.reshape(batch, seq, D_MODEL)


def _trunk(x, layers):
    for i, p in enumerate(layers):
        x = _layer_a(x, p, i) if i % 2 == 0 else _layer_b(x, p)
    return x


def kernel(x_prompt, x_sample, w_in_0, lam_q1_0, lam_k1_0, lam_q2_0, lam_k2_0, subln_g_0, w_out_0, ln_g_0, ln_b_0, w_in_1, w_out_1, ln_g_1, ln_b_1, w_in_2, lam_q1_2, lam_k1_2, lam_q2_2, lam_k2_2, subln_g_2, w_out_2, ln_g_2, ln_b_2, w_in_3, w_out_3, ln_g_3, ln_b_3):
    layers = [
        dict(w_in=w_in_0, lam_q1=lam_q1_0, lam_k1=lam_k1_0, lam_q2=lam_q2_0, lam_k2=lam_k2_0,
             subln_g=subln_g_0, w_out=w_out_0, ln_g=ln_g_0, ln_b=ln_b_0),
        dict(w_in=w_in_1, w_out=w_out_1, ln_g=ln_g_1, ln_b=ln_b_1),
        dict(w_in=w_in_2, lam_q1=lam_q1_2, lam_k1=lam_k1_2, lam_q2=lam_q2_2, lam_k2=lam_k2_2,
             subln_g=subln_g_2, w_out=w_out_2, ln_g=ln_g_2, ln_b=ln_b_2),
        dict(w_in=w_in_3, w_out=w_out_3, ln_g=ln_g_3, ln_b=ln_b_3),
    ]
    return (_trunk(x_prompt, layers), _trunk(x_sample, layers))
```

```python
import functools
import math

import jax
import jax.numpy as jnp
from jax import lax
from jax.experimental import pallas as pl
from jax.experimental.pallas import tpu as pltpu

F32 = jnp.float32
BF16 = jnp.bfloat16

D_MODEL = 1024
DEPTH = 4
ROPE_THETA = 500000.0
LN_EPS = 1e-5
ALPHA = (2.0 * DEPTH) ** 0.25
HEAD_DIM = 64
LANES = 128
N_PAIRS = D_MODEL // LANES
SECTION_PLANES = D_MODEL // LANES
B_GROUPS = ((128, 1), (512, 4), (2048, 16))
RADIUS = 64
TILE = 2048
HALO = 64
QBLK = 128
MASKED = -1e30
GROUP = 4
VMEM_LIMIT = 56 * 1024 * 1024


def _cparams(sem, flags=None):
    return pltpu.CompilerParams(dimension_semantics=sem, vmem_limit_bytes=VMEM_LIMIT, flags=flags)


def _rope_tables(seq, dil, tm):
    half = HEAD_DIM // 8
    pos = jnp.arange(seq, dtype=jnp.int32).reshape(seq // tm, tm // dil, dil).transpose(0, 2, 1).reshape(seq)
    dim = jnp.arange(LANES, dtype=jnp.int32) % HEAD_DIM
    rotated = dim < 2 * half
    inv = jnp.where(rotated, ROPE_THETA ** (-(dim % half).astype(F32) / half), 0.0)
    ang = pos.astype(F32)[:, None] * inv[None, :]
    sin = jnp.sin(ang)
    c = jnp.cos(ang)
    sa = sin * jnp.where(dim < half, -1.0, 0.0)[None, :]
    sb = sin * jnp.where(rotated & (dim >= half), 1.0, 0.0)[None, :]
    return c, sa, sb


X_SLABS = D_MODEL // LANES


ROPE_COLS = 2 * D_MODEL


def _proj_kernel(*refs, dil, tm, sub_n, n_out):
    x_refs = refs[:X_SLABS]
    w_ref, c_ref, sa_ref, sb_ref, o_ref, xs_ref = refs[X_SLABS:]
    res_rows = tm // dil

    for c in range(X_SLABS):
        for r in range(dil):
            src = x_refs[c][pl.ds(r, res_rows, stride=dil), :] if dil > 1 else x_refs[c][...]
            xs_ref[r * res_rows:(r + 1) * res_rows, c * LANES:(c + 1) * LANES] = src.astype(BF16)

    for sd in range(n_out // sub_n):
        acc = jnp.dot(xs_ref[...], w_ref[:, sd * sub_n:(sd + 1) * sub_n], preferred_element_type=F32)
        for t in range(sub_n // LANES):
            a = acc[:, t * LANES:(t + 1) * LANES]
            if sd * sub_n < ROPE_COLS:
                a = a * c_ref[...] + pltpu.roll(a, LANES - 8, 1) * sa_ref[...] + pltpu.roll(a, 8, 1) * sb_ref[...]
            ab = a.astype(BF16)
            for r in range(dil):
                o_ref[sd * (sub_n // LANES) + t, r] = ab[r * res_rows:(r + 1) * res_rows]


def _project(x2d, w, *, seq, dil, tile, tm=512, sub_n=512):
    n = x2d.shape[0]
    n_out = w.shape[1]
    per_tile = tile // tm
    n_seq_tiles = seq // tm
    tabs = _rope_tables(seq, dil, tm)
    kern = functools.partial(_proj_kernel, dil=dil, tm=tm, sub_n=sub_n, n_out=n_out)
    tab_spec = pl.BlockSpec((tm, LANES), lambda i: (i % n_seq_tiles, 0))
    x_specs = [pl.BlockSpec((tm, LANES), lambda i, c=c: (i, c)) for c in range(X_SLABS)]
    return pl.pallas_call(
        kern,
        grid=(n // tm,),
        in_specs=x_specs + [
            pl.BlockSpec((D_MODEL, n_out), lambda i: (0, 0)),
            tab_spec, tab_spec, tab_spec,
        ],
        out_specs=pl.BlockSpec((n_out // LANES, None, dil, tm // dil, LANES),
                               lambda i: (0, i // per_tile, 0, i % per_tile, 0)),
        out_shape=jax.ShapeDtypeStruct((n_out // LANES, n // tile, dil, tile // dil, LANES), BF16),
        scratch_shapes=[pltpu.VMEM((tm, D_MODEL), BF16)],
        compiler_params=_cparams(("arbitrary",)),
    )(*([x2d] * X_SLABS), w, *tabs)


N_SCORE_BUFS = 2


def _silu(g):
    return g * (1.0 / (1.0 + jnp.exp(-g)))


def _attn_a_kernel(lamv_ref, subg_ref, q_ref, k_ref, v_ref, g_ref, o_ref,
                   qt_ref, vt_ref, s0_ref, s1_ref, m_ref, l_ref, acc_ref, *, lam_init, bq, bk, nkv):
    @pl.when(pl.program_id(2) == 0)
    def _():
        for j in range(nkv):
            vt_ref[:, j * bk:(j + 1) * bk] = v_ref[j * bk:(j + 1) * bk, :].astype(F32).T.astype(BF16)

    qt = q_ref[...].astype(F32).T
    dim = lax.broadcasted_iota(jnp.int32, qt.shape, 0)
    qt_ref[:, 0:bq] = jnp.where(dim < HEAD_DIM, qt, 0.0).astype(BF16)
    qt_ref[:, bq:2 * bq] = jnp.where(dim >= HEAD_DIM, qt, 0.0).astype(BF16)
    m_ref[...] = jnp.full(m_ref.shape, -jnp.inf, F32)
    l_ref[...] = jnp.zeros(l_ref.shape, F32)
    acc_ref[...] = jnp.zeros(acc_ref.shape, F32)

    def scores(j, s_ref):
        kc = k_ref[pl.ds(pl.multiple_of(j * bk, bk), bk), :]
        s_ref[...] = jnp.dot(kc, qt_ref[...], preferred_element_type=F32)

    def update(j, s_ref):
        m_prev = m_ref[...]
        m_new = jnp.maximum(m_prev, jnp.max(s_ref[...], axis=0, keepdims=True))
        alpha = jnp.exp2(m_prev - m_new)
        p = jnp.exp2(s_ref[...] - m_new)
        l_ref[...] = alpha * l_ref[...] + jnp.sum(p, axis=0, keepdims=True)
        pv = jnp.dot(vt_ref[:, j * bk:(j + 1) * bk], p.astype(BF16), preferred_element_type=F32)
        acc_ref[...] = alpha * acc_ref[...] + pv
        m_ref[...] = m_new

    bufs = (s0_ref, s1_ref)
    scores(0, bufs[0])
    for j in range(nkv - 1):
        scores(j + 1, bufs[(j + 1) % N_SCORE_BUFS])
        update(j, bufs[j % N_SCORE_BUFS])
    update(nkv - 1, bufs[(nkv - 1) % N_SCORE_BUFS])

    lv = lamv_ref[...]
    lam = (jnp.exp(jnp.sum(lv[0:1] * lv[1:2], axis=1, keepdims=True))
           - jnp.exp(jnp.sum(lv[2:3] * lv[3:4], axis=1, keepdims=True)) + lam_init)
    ot = acc_ref[...] / l_ref[...]
    odt = ot[:, 0:bq] - lam * ot[:, bq:2 * bq]
    odt = odt * lax.rsqrt(jnp.mean(odt * odt, axis=0, keepdims=True) + LN_EPS)
    od = odt.T * subg_ref[...] * (1.0 - lam_init)
    o_ref[...] = (od * _silu(g_ref[...].astype(F32))).astype(BF16)


def _attn_a(planes, lamv, subg, *, batch, seq, lam_init, bq=512, bk=1024):
    pl4 = planes.reshape(4 * SECTION_PLANES, batch, seq, LANES)
    nkv = seq // bk
    assert seq % bk == 0 and seq % bq == 0
    kern = functools.partial(_attn_a_kernel, lam_init=lam_init, bq=bq, bk=bk, nkv=nkv)

    def q_rows(s):
        return pl.BlockSpec((None, None, bq, LANES), lambda b, h, qi: (s * SECTION_PLANES + h, b, qi, 0))

    def all_rows(s):
        return pl.BlockSpec((None, None, seq, LANES), lambda b, h, qi: (s * SECTION_PLANES + h, b, 0, 0))

    return pl.pallas_call(
        kern,
        grid=(batch, N_PAIRS, seq // bq),
        in_specs=[
            pl.BlockSpec((4, HEAD_DIM), lambda b, h, qi: (0, 0)),
            pl.BlockSpec((1, LANES), lambda b, h, qi: (0, 0)),
            q_rows(0), all_rows(1), all_rows(2), q_rows(3),
        ],
        out_specs=pl.BlockSpec((None, bq, LANES), lambda b, h, qi: (b, qi, h)),
        out_shape=jax.ShapeDtypeStruct((batch, seq, D_MODEL), BF16),
        scratch_shapes=[
            pltpu.VMEM((LANES, 2 * bq), BF16),
            pltpu.VMEM((LANES, seq), BF16),
        ] + [pltpu.VMEM((bk, 2 * bq), F32)] * N_SCORE_BUFS + [
            pltpu.VMEM((1, 2 * bq), F32),
            pltpu.VMEM((1, 2 * bq), F32),
            pltpu.VMEM((LANES, 2 * bq), F32),
        ],
        compiler_params=_cparams(("arbitrary", "arbitrary", "arbitrary")),
    )(lamv, subg, pl4, pl4, pl4, pl4)


def _attn_b_group(gi, dil, seq, q_ref, k_ref, kp_ref, kn_ref, v_ref, vp_ref, vn_ref,
                  kw_ref, vw_ref, og_ref, mg_ref, lg_ref, bias_ref):
    rows = TILE // dil
    nb64 = rows // HALO
    tile_i = pl.program_id(2)
    length = seq // dil

    kw_ref[0:dil, 0:HALO, :] = kp_ref[...]
    vw_ref[0:dil, 0:HALO, :] = vp_ref[...]
    for n in range(nb64):
        kw_ref[0:dil, HALO + n * HALO:2 * HALO + n * HALO, :] = k_ref[:, n]
        vw_ref[0:dil, HALO + n * HALO:2 * HALO + n * HALO, :] = v_ref[:, n]
    kw_ref[0:dil, HALO + rows:2 * HALO + rows, :] = kn_ref[...]
    vw_ref[0:dil, HALO + rows:2 * HALO + rows, :] = vn_ref[...]

    win = QBLK + 2 * HALO
    lane_q = lax.broadcasted_iota(jnp.int32, (QBLK, LANES), 1)
    lo_q = lane_q < HEAD_DIM
    blocks = rows // QBLK
    unroll = max(1, 8 // blocks)

    def block_scores(r, blk):
        q2 = q_ref[r, 2 * blk:2 * blk + 2].reshape(QBLK, LANES)
        zero = jnp.zeros_like(q2)
        qs = jnp.concatenate([jnp.where(lo_q, q2, zero), jnp.where(lo_q, zero, q2)], axis=0)
        kwin = kw_ref[r, blk * QBLK:blk * QBLK + win, :]
        s = lax.dot_general(qs, kwin, (((1,), (1,)), ((), ())), preferred_element_type=F32)
        first = tile_i * rows + blk * QBLK
        edge = (first == 0).astype(jnp.int32) + 2 * (first + QBLK == length).astype(jnp.int32)
        return s + bias_ref[edge]

    def block_softmax(s):
        m = jnp.max(s, axis=1, keepdims=True)
        p = jnp.exp2(s - m)
        return p.astype(BF16), m, jnp.sum(p, axis=1, keepdims=True)

    def per_head(a):
        return jnp.where(lo_q, jnp.broadcast_to(a[0:QBLK], (QBLK, LANES)),
                         jnp.broadcast_to(a[QBLK:2 * QBLK], (QBLK, LANES)))

    def block_output(r, blk, pb, m, l):
        vwin = vw_ref[r, blk * QBLK:blk * QBLK + win, :]
        o0 = jnp.dot(pb[0:QBLK], vwin, preferred_element_type=F32)
        o1 = jnp.dot(pb[QBLK:2 * QBLK], vwin, preferred_element_type=F32)
        tok = pl.ds(r + blk * QBLK * dil, QBLK, stride=dil) if dil > 1 else pl.ds(blk * QBLK, QBLK)
        og_ref[gi, tok, :] = jnp.where(lo_q, o0, o1)
        mg_ref[gi, tok, :] = per_head(m)
        lg_ref[gi, tok, :] = per_head(l)

    def body(it, carry):
        items = [(it * unroll + u, blk) for u in range(unroll) for blk in range(blocks)]
        for g0 in range(0, len(items), GROUP):
            group = items[g0:g0 + GROUP]
            scores = [block_scores(r, blk) for r, blk in group]
            soft = [block_softmax(s) for s in scores]
            for (r, blk), (pb, m, l) in zip(group, soft):
                block_output(r, blk, pb, m, l)
        return carry

    if dil == 1:
        body(0, 0)
    else:
        lax.fori_loop(0, dil // unroll, body, 0)


def _attn_b_kernel(*refs, seq):
    n_in = 7 * len(B_GROUPS) + 1
    gate_ref = refs[n_in - 1]
    o_ref = refs[n_in]
    kw_ref, vw_ref, og_ref, mg_ref, lg_ref, bias_ref = refs[n_in + 1:]

    win = QBLK + 2 * HALO
    row_i = lax.broadcasted_iota(jnp.int32, (2 * QBLK, win), 0)
    col_i = lax.broadcasted_iota(jnp.int32, (2 * QBLK, win), 1)
    qpos = jnp.where(row_i >= QBLK, row_i - QBLK, row_i)
    band = jnp.where(jnp.abs(col_i - HALO - qpos) <= RADIUS, 0.0, MASKED)
    low = jnp.where(col_i < HALO, MASKED, 0.0)
    high = jnp.where(col_i >= QBLK + HALO, MASKED, 0.0)
    bias_ref[0] = band
    bias_ref[1] = band + low
    bias_ref[2] = band + high
    bias_ref[3] = band + low + high

    for gi, (_, dil) in enumerate(B_GROUPS):
        _attn_b_group(gi, dil, seq, *refs[7 * gi:7 * gi + 7], kw_ref, vw_ref, og_ref, mg_ref, lg_ref, bias_ref)
    m0, m1, m2 = mg_ref[0], mg_ref[1], mg_ref[2]
    mx = jnp.maximum(jnp.maximum(m0, m1), m2)
    e0, e1, e2 = jnp.exp2(m0 - mx), jnp.exp2(m1 - mx), jnp.exp2(m2 - mx)
    den = e0 * lg_ref[0] + e1 * lg_ref[1] + e2 * lg_ref[2]
    o = (e0 * og_ref[0] + e1 * og_ref[1] + e2 * og_ref[2]) / den
    o_ref[...] = (o * _silu(gate_ref[...].astype(F32))).astype(BF16)


def _attn_b(group_planes, gate_planes, gate_base, *, batch, seq):
    n_tiles = seq // TILE
    in_specs, args = [], []
    for (_, dil), planes in zip(B_GROUPS, group_planes):
        rows = TILE // dil
        nb64 = rows // HALO
        view = planes.reshape(planes.shape[0], batch, n_tiles, dil, nb64, HALO, LANES)

        def main(s):
            return pl.BlockSpec((None, None, None, dil, nb64, HALO, LANES),
                                lambda b, h, i, s=s: (s * SECTION_PLANES + h, b, i, 0, 0, 0, 0))

        def prev(s, nb64=nb64):
            return pl.BlockSpec((None, None, None, dil, None, HALO, LANES),
                                lambda b, h, i, s=s: (s * SECTION_PLANES + h, b, jnp.maximum(i - 1, 0), 0, nb64 - 1, 0, 0))

        def nxt(s):
            return pl.BlockSpec((None, None, None, dil, None, HALO, LANES),
                                lambda b, h, i, s=s: (s * SECTION_PLANES + h, b, jnp.minimum(i + 1, n_tiles - 1), 0, 0, 0, 0))

        in_specs += [main(0), main(1), prev(1), nxt(1), main(2), prev(2), nxt(2)]
        args += [view] * 7
    gate_view = gate_planes.reshape(gate_planes.shape[0], batch, seq, LANES)
    in_specs.append(pl.BlockSpec((None, None, TILE, LANES), lambda b, h, i: (gate_base + h, b, i, 0)))
    args.append(gate_view)
    max_dil = max(d for _, d in B_GROUPS)
    return pl.pallas_call(
        functools.partial(_attn_b_kernel, seq=seq),
        grid=(batch, N_PAIRS, n_tiles),
        in_specs=in_specs,
        out_specs=pl.BlockSpec((None, TILE, LANES), lambda b, h, i: (b, i, h)),
        out_shape=jax.ShapeDtypeStruct((batch, seq, D_MODEL), BF16),
        scratch_shapes=[
            pltpu.VMEM((max_dil, TILE + 2 * HALO, LANES), BF16),
            pltpu.VMEM((max_dil, TILE + 2 * HALO, LANES), BF16),
            pltpu.VMEM((len(B_GROUPS), TILE, LANES), F32),
            pltpu.VMEM((len(B_GROUPS), TILE, LANES), F32),
            pltpu.VMEM((len(B_GROUPS), TILE, LANES), F32),
            pltpu.VMEM((4, 2 * QBLK, QBLK + 2 * HALO), F32),
        ],
        compiler_params=_cparams(("arbitrary", "arbitrary", "arbitrary")),
    )(*args)


def _out_ln_kernel(y_ref, w_ref, x_ref, g_ref, b_ref, o_ref, *, tm, rc):
    def body(ci, carry):
        rows = pl.ds(pl.multiple_of(ci * rc, rc), rc)
        f = jnp.dot(y_ref[rows, :], w_ref[...], preferred_element_type=F32)
        z = ALPHA * x_ref[rows, :] + f
        mu = jnp.mean(z, axis=1, keepdims=True)
        zc = z - mu
        var = jnp.mean(zc * zc, axis=1, keepdims=True)
        o_ref[rows, :] = zc * lax.rsqrt(var + LN_EPS) * g_ref[...] + b_ref[...]
        return carry

    lax.fori_loop(0, tm // rc, body, 0)


def _out_ln(y2d, w, x2d, g, b, *, tm=512, rc=256):
    n = x2d.shape[0]
    return pl.pallas_call(
        functools.partial(_out_ln_kernel, tm=tm, rc=rc),
        grid=(n // tm,),
        in_specs=[
            pl.BlockSpec((tm, D_MODEL), lambda i: (i, 0)),
            pl.BlockSpec((D_MODEL, D_MODEL), lambda i: (0, 0)),
            pl.BlockSpec((tm, D_MODEL), lambda i: (i, 0)),
            pl.BlockSpec((1, D_MODEL), lambda i: (0, 0)),
            pl.BlockSpec((1, D_MODEL), lambda i: (0, 0)),
        ],
        out_specs=pl.BlockSpec((tm, D_MODEL), lambda i: (i, 0)),
        out_shape=jax.ShapeDtypeStruct((n, D_MODEL), F32),
        compiler_params=_cparams(("arbitrary",)),
    )(y2d, w, x2d, g.reshape(1, D_MODEL), b.reshape(1, D_MODEL))


Q_SCALE_LOG2 = math.log2(math.e) / math.sqrt(HEAD_DIM)


def _scale_q(w):
    return jnp.concatenate([w[:, :D_MODEL] * Q_SCALE_LOG2, w[:, D_MODEL:]], axis=1).astype(BF16)


def _layer_a(x, p, layer_idx):
    batch, seq, _ = x.shape
    x2d = x.reshape(batch * seq, D_MODEL)
    w_in = _scale_q(p["w_in"])
    planes = _project(x2d, w_in, seq=seq, dil=1, tile=1024)
    lam_init = 0.8 - 0.6 * math.exp(-0.3 * layer_idx)
    lamv = jnp.stack([p["lam_q1"], p["lam_k1"], p["lam_q2"], p["lam_k2"]]).astype(F32)
    y = _attn_a(planes, lamv, p["subln_g"].astype(F32).reshape(1, LANES), batch=batch, seq=seq, lam_init=lam_init)
    out = _out_ln(y.reshape(batch * seq, D_MODEL), p["w_out"].astype(BF16), x2d, p["ln_g"], p["ln_b"])
    return out.reshape(batch, seq, D_MODEL)


def _layer_b(x, p):
    batch, seq, _ = x.shape
    x2d = x.reshape(batch * seq, D_MODEL)
    w_in = p["w_in"]
    group_planes = []
    gate_planes = None
    for gi, (_, dil) in enumerate(B_GROUPS):
        w = w_in[:, 3 * gi * D_MODEL:3 * (gi + 1) * D_MODEL]
        if dil == 1:
            w = jnp.concatenate([w, w_in[:, 3 * len(B_GROUPS) * D_MODEL:]], axis=1)
        planes = _project(x2d, _scale_q(w), seq=seq, dil=dil, tile=TILE)
        if dil == 1:
            gate_planes = planes
        group_planes.append(planes)
    y = _attn_b(group_planes, gate_planes, 3 * SECTION_PLANES, batch=batch, seq=seq)
    out = _out_ln(y.reshape(batch * seq, D_MODEL), p["w_out"].astype(BF16), x2d, p["ln_g"], p["ln_b"])
    return out.reshape(batch, seq, D_MODEL)


def _trunk(x, layers):
    for i, p in enumerate(layers):
        x = _layer_a(x, p, i) if i % 2 == 0 else _layer_b(x, p)
    return x


def kernel(x_prompt, x_sample, w_in_0, lam_q1_0, lam_k1_0, lam_q2_0, lam_k2_0, subln_g_0, w_out_0, ln_g_0, ln_b_0, w_in_1, w_out_1, ln_g_1, ln_b_1, w_in_2, lam_q1_2, lam_k1_2, lam_q2_2, lam_k2_2, subln_g_2, w_out_2, ln_g_2, ln_b_2, w_in_3, w_out_3, ln_g_3, ln_b_3):
    layers = [
        dict(w_in=w_in_0, lam_q1=lam_q1_0, lam_k1=lam_k1_0, lam_q2=lam_q2_0, lam_k2=lam_k2_0,
             subln_g=subln_g_0, w_out=w_out_0, ln_g=ln_g_0, ln_b=ln_b_0),
        dict(w_in=w_in_1, w_out=w_out_1, ln_g=ln_g_1, ln_b=ln_b_1),
        dict(w_in=w_in_2, lam_q1=lam_q1_2, lam_k1=lam_k1_2, lam_q2=lam_q2_2, lam_k2=lam_k2_2,
             subln_g=subln_g_2, w_out=w_out_2, ln_g=ln_g_2, ln_b=ln_b_2),
        dict(w_in=w_in_3, w_out=w_out_3, ln_g=ln_g_3, ln_b=ln_b_3),
    ]
    return (_trunk(x_prompt, layers), _trunk(x_sample, layers))
```

```python
import functools
import math

import jax
import jax.numpy as jnp
from jax import lax
from jax.experimental import pallas as pl
from jax.experimental.pallas import tpu as pltpu

F32 = jnp.float32
BF16 = jnp.bfloat16

D_MODEL = 1024
DEPTH = 4
ROPE_THETA = 500000.0
LN_EPS = 1e-5
ALPHA = (2.0 * DEPTH) ** 0.25
HEAD_DIM = 64
LANES = 128
N_PAIRS = D_MODEL // LANES
SECTION_PLANES = D_MODEL // LANES
B_GROUPS = ((128, 1), (512, 4), (2048, 16))
RADIUS = 64
TILE = 2048
HALO = 64
QBLK = 128
MASKED = -1e30
GROUP = 4
VMEM_LIMIT = 56 * 1024 * 1024


def _cparams(sem, flags=None):
    return pltpu.CompilerParams(dimension_semantics=sem, vmem_limit_bytes=VMEM_LIMIT, flags=flags)


def _rope_tables(seq, dil, tm):
    half = HEAD_DIM // 8
    pos = jnp.arange(seq, dtype=jnp.int32).reshape(seq // tm, tm // dil, dil).transpose(0, 2, 1).reshape(seq)
    dim = jnp.arange(LANES, dtype=jnp.int32) % HEAD_DIM
    rotated = dim < 2 * half
    inv = jnp.where(rotated, ROPE_THETA ** (-(dim % half).astype(F32) / half), 0.0)
    ang = pos.astype(F32)[:, None] * inv[None, :]
    sin = jnp.sin(ang)
    c = jnp.cos(ang)
    sa = sin * jnp.where(dim < half, -1.0, 0.0)[None, :]
    sb = sin * jnp.where(rotated & (dim >= half), 1.0, 0.0)[None, :]
    return c, sa, sb


X_SLABS = D_MODEL // LANES


ROPE_COLS = 2 * D_MODEL


def _proj_kernel(*refs, dil, tm, sub_n, n_out):
    x_refs = refs[:X_SLABS]
    w_ref, c_ref, sa_ref, sb_ref, o_ref, xs_ref = refs[X_SLABS:]
    res_rows = tm // dil

    for c in range(X_SLABS):
        for r in range(dil):
            src = x_refs[c][pl.ds(r, res_rows, stride=dil), :] if dil > 1 else x_refs[c][...]
            xs_ref[r * res_rows:(r + 1) * res_rows, c * LANES:(c + 1) * LANES] = src.astype(BF16)

    for sd in range(n_out // sub_n):
        acc = jnp.dot(xs_ref[...], w_ref[:, sd * sub_n:(sd + 1) * sub_n], preferred_element_type=F32)
        for t in range(sub_n // LANES):
            a = acc[:, t * LANES:(t + 1) * LANES]
            if sd * sub_n < ROPE_COLS:
                a = a * c_ref[...] + pltpu.roll(a, LANES - 8, 1) * sa_ref[...] + pltpu.roll(a, 8, 1) * sb_ref[...]
            ab = a.astype(BF16)
            for r in range(dil):
                o_ref[sd * (sub_n // LANES) + t, r] = ab[r * res_rows:(r + 1) * res_rows]


def _project(x2d, w, *, seq, dil, tile, tm=512, sub_n=512):
    n = x2d.shape[0]
    n_out = w.shape[1]
    per_tile = tile // tm
    n_seq_tiles = seq // tm
    tabs = _rope_tables(seq, dil, tm)
    kern = functools.partial(_proj_kernel, dil=dil, tm=tm, sub_n=sub_n, n_out=n_out)
    tab_spec = pl.BlockSpec((tm, LANES), lambda i: (i % n_seq_tiles, 0))
    x_specs = [pl.BlockSpec((tm, LANES), lambda i, c=c: (i, c)) for c in range(X_SLABS)]
    return pl.pallas_call(
        kern,
        grid=(n // tm,),
        in_specs=x_specs + [
            pl.BlockSpec((D_MODEL, n_out), lambda i: (0, 0)),
            tab_spec, tab_spec, tab_spec,
        ],
        out_specs=pl.BlockSpec((n_out // LANES, None, dil, tm // dil, LANES),
                               lambda i: (0, i // per_tile, 0, i % per_tile, 0)),
        out_shape=jax.ShapeDtypeStruct((n_out // LANES, n // tile, dil, tile // dil, LANES), BF16),
        scratch_shapes=[pltpu.VMEM((tm, D_MODEL), BF16)],
        compiler_params=_cparams(("arbitrary",)),
    )(*([x2d] * X_SLABS), w, *tabs)


N_SCORE_BUFS = 2


def _silu(g):
    return g * (1.0 / (1.0 + jnp.exp(-g)))


def _attn_a_kernel(lamv_ref, subg_ref, q_ref, k_ref, v_ref, g_ref, o_ref,
                   qt_ref, vt_ref, s0_ref, s1_ref, m_ref, l_ref, acc_ref, *, lam_init, bq, bk, nkv):
    @pl.when(pl.program_id(2) == 0)
    def _():
        for j in range(nkv):
            vt_ref[:, j * bk:(j + 1) * bk] = v_ref[j * bk:(j + 1) * bk, :].astype(F32).T.astype(BF16)

    qt = q_ref[...].astype(F32).T
    dim = lax.broadcasted_iota(jnp.int32, qt.shape, 0)
    qt_ref[:, 0:bq] = jnp.where(dim < HEAD_DIM, qt, 0.0).astype(BF16)
    qt_ref[:, bq:2 * bq] = jnp.where(dim >= HEAD_DIM, qt, 0.0).astype(BF16)
    m_ref[...] = jnp.full(m_ref.shape, -jnp.inf, F32)
    l_ref[...] = jnp.zeros(l_ref.shape, F32)
    acc_ref[...] = jnp.zeros(acc_ref.shape, F32)

    def scores(j, s_ref):
        kc = k_ref[pl.ds(pl.multiple_of(j * bk, bk), bk), :]
        s_ref[...] = jnp.dot(kc, qt_ref[...], preferred_element_type=F32)

    def update(j, s_ref):
        m_prev = m_ref[...]
        m_new = jnp.maximum(m_prev, jnp.max(s_ref[...], axis=0, keepdims=True))
        alpha = jnp.exp2(m_prev - m_new)
        p = jnp.exp2(s_ref[...] - m_new)
        l_ref[...] = alpha * l_ref[...] + jnp.sum(p, axis=0, keepdims=True)
        pv = jnp.dot(vt_ref[:, j * bk:(j + 1) * bk], p.astype(BF16), preferred_element_type=F32)
        acc_ref[...] = alpha * acc_ref[...] + pv
        m_ref[...] = m_new

    bufs = (s0_ref, s1_ref)
    scores(0, bufs[0])
    for j in range(nkv - 1):
        scores(j + 1, bufs[(j + 1) % N_SCORE_BUFS])
        update(j, bufs[j % N_SCORE_BUFS])
    update(nkv - 1, bufs[(nkv - 1) % N_SCORE_BUFS])

    lv = lamv_ref[...]
    lam = (jnp.exp(jnp.sum(lv[0:1] * lv[1:2], axis=1, keepdims=True))
           - jnp.exp(jnp.sum(lv[2:3] * lv[3:4], axis=1, keepdims=True)) + lam_init)
    ot = acc_ref[...] / l_ref[...]
    odt = ot[:, 0:bq] - lam * ot[:, bq:2 * bq]
    odt = odt * lax.rsqrt(jnp.mean(odt * odt, axis=0, keepdims=True) + LN_EPS)
    od = odt.T * subg_ref[...] * (1.0 - lam_init)
    o_ref[...] = (od * _silu(g_ref[...].astype(F32))).astype(BF16)


def _attn_a(planes, lamv, subg, *, batch, seq, lam_init, bq=512, bk=1024):
    pl4 = planes.reshape(4 * SECTION_PLANES, batch, seq, LANES)
    nkv = seq // bk
    assert seq % bk == 0 and seq % bq == 0
    kern = functools.partial(_attn_a_kernel, lam_init=lam_init, bq=bq, bk=bk, nkv=nkv)

    def q_rows(s):
        return pl.BlockSpec((None, None, bq, LANES), lambda b, h, qi: (s * SECTION_PLANES + h, b, qi, 0))

    def all_rows(s):
        return pl.BlockSpec((None, None, seq, LANES), lambda b, h, qi: (s * SECTION_PLANES + h, b, 0, 0))

    return pl.pallas_call(
        kern,
        grid=(batch, N_PAIRS, seq // bq),
        in_specs=[
            pl.BlockSpec((4, HEAD_DIM), lambda b, h, qi: (0, 0)),
            pl.BlockSpec((1, LANES), lambda b, h, qi: (0, 0)),
            q_rows(0), all_rows(1), all_rows(2), q_rows(3),
        ],
        out_specs=pl.BlockSpec((None, bq, LANES), lambda b, h, qi: (b, qi, h)),
        out_shape=jax.ShapeDtypeStruct((batch, seq, D_MODEL), BF16),
        scratch_shapes=[
            pltpu.VMEM((LANES, 2 * bq), BF16),
            pltpu.VMEM((LANES, seq), BF16),
        ] + [pltpu.VMEM((bk, 2 * bq), F32)] * N_SCORE_BUFS + [
            pltpu.VMEM((1, 2 * bq), F32),
            pltpu.VMEM((1, 2 * bq), F32),
            pltpu.VMEM((LANES, 2 * bq), F32),
        ],
        compiler_params=_cparams(("arbitrary", "arbitrary", "arbitrary")),
    )(lamv, subg, pl4, pl4, pl4, pl4)


def _attn_b_group(gi, dil, seq, q_ref, k_ref, kp_ref, kn_ref, v_ref, vp_ref, vn_ref,
                  kw_ref, vw_ref, og_ref, mg_ref, lg_ref, bias_ref):
    rows = TILE // dil
    nb64 = rows // HALO
    tile_i = pl.program_id(2)
    length = seq // dil

    kw_ref[0:dil, 0:HALO, :] = kp_ref[...]
    vw_ref[0:dil, 0:HALO, :] = vp_ref[...]
    for n in range(nb64):
        kw_ref[0:dil, HALO + n * HALO:2 * HALO + n * HALO, :] = k_ref[:, n]
        vw_ref[0:dil, HALO + n * HALO:2 * HALO + n * HALO, :] = v_ref[:, n]
    kw_ref[0:dil, HALO + rows:2 * HALO + rows, :] = kn_ref[...]
    vw_ref[0:dil, HALO + rows:2 * HALO + rows, :] = vn_ref[...]

    win = QBLK + 2 * HALO
    lane_q = lax.broadcasted_iota(jnp.int32, (QBLK, LANES), 1)
    lo_q = lane_q < HEAD_DIM
    blocks = rows // QBLK
    unroll = max(1, 8 // blocks)

    def block_scores(r, blk):
        q2 = q_ref[r, 2 * blk:2 * blk + 2].reshape(QBLK, LANES)
        zero = jnp.zeros_like(q2)
        qs = jnp.concatenate([jnp.where(lo_q, q2, zero), jnp.where(lo_q, zero, q2)], axis=0)
        kwin = kw_ref[r, blk * QBLK:blk * QBLK + win, :]
        s = lax.dot_general(qs, kwin, (((1,), (1,)), ((), ())), preferred_element_type=F32)
        first = tile_i * rows + blk * QBLK
        edge = (first == 0).astype(jnp.int32) + 2 * (first + QBLK == length).astype(jnp.int32)
        return s + bias_ref[edge]

    def block_softmax(s):
        m = jnp.max(s, axis=1, keepdims=True)
        p = jnp.exp2(s - m)
        return p.astype(BF16), m, jnp.sum(p, axis=1, keepdims=True)

    def per_head(a):
        return jnp.where(lo_q, jnp.broadcast_to(a[0:QBLK], (QBLK, LANES)),
                         jnp.broadcast_to(a[QBLK:2 * QBLK], (QBLK, LANES)))

    def block_output(r, blk, pb, m, l):
        vwin = vw_ref[r, blk * QBLK:blk * QBLK + win, :]
        o0 = jnp.dot(pb[0:QBLK], vwin, preferred_element_type=F32)
        o1 = jnp.dot(pb[QBLK:2 * QBLK], vwin, preferred_element_type=F32)
        tok = pl.ds(r + blk * QBLK * dil, QBLK, stride=dil) if dil > 1 else pl.ds(blk * QBLK, QBLK)
        og_ref[gi, tok, :] = jnp.where(lo_q, o0, o1)
        mg_ref[gi, tok, :] = per_head(m)
        lg_ref[gi, tok, :] = per_head(l)

    def body(it, carry):
        items = [(it * unroll + u, blk) for u in range(unroll) for blk in range(blocks)]
        for g0 in range(0, len(items), GROUP):
            group = items[g0:g0 + GROUP]
            scores = [block_scores(r, blk) for r, blk in group]
            soft = [block_softmax(s) for s in scores]
            for (r, blk), (pb, m, l) in zip(group, soft):
                block_output(r, blk, pb, m, l)
        return carry

    if dil == 1:
        body(0, 0)
    else:
        lax.fori_loop(0, dil // unroll, body, 0)


def _attn_b_kernel(*refs, seq):
    n_in = 7 * len(B_GROUPS) + 1
    gate_ref = refs[n_in - 1]
    o_ref = refs[n_in]
    kw_ref, vw_ref, og_ref, mg_ref, lg_ref, bias_ref = refs[n_in + 1:]

    @pl.when((pl.program_id(0) == 0) & (pl.program_id(1) == 0) & (pl.program_id(2) == 0))
    def _():
        win = QBLK + 2 * HALO
        row_i = lax.broadcasted_iota(jnp.int32, (2 * QBLK, win), 0)
        col_i = lax.broadcasted_iota(jnp.int32, (2 * QBLK, win), 1)
        qpos = jnp.where(row_i >= QBLK, row_i - QBLK, row_i)
        band = jnp.where(jnp.abs(col_i - HALO - qpos) <= RADIUS, 0.0, MASKED)
        low = jnp.where(col_i < HALO, MASKED, 0.0)
        high = jnp.where(col_i >= QBLK + HALO, MASKED, 0.0)
        bias_ref[0] = band
        bias_ref[1] = band + low
        bias_ref[2] = band + high
        bias_ref[3] = band + low + high

    for gi, (_, dil) in enumerate(B_GROUPS):
        _attn_b_group(gi, dil, seq, *refs[7 * gi:7 * gi + 7], kw_ref, vw_ref, og_ref, mg_ref, lg_ref, bias_ref)
    m0, m1, m2 = mg_ref[0], mg_ref[1], mg_ref[2]
    mx = jnp.maximum(jnp.maximum(m0, m1), m2)
    e0, e1, e2 = jnp.exp2(m0 - mx), jnp.exp2(m1 - mx), jnp.exp2(m2 - mx)
    den = e0 * lg_ref[0] + e1 * lg_ref[1] + e2 * lg_ref[2]
    o = (e0 * og_ref[0] + e1 * og_ref[1] + e2 * og_ref[2]) / den
    o_ref[...] = (o * _silu(gate_ref[...].astype(F32))).astype(BF16)


def _attn_b(group_planes, gate_planes, gate_base, *, batch, seq):
    n_tiles = seq // TILE
    in_specs, args = [], []
    for (_, dil), planes in zip(B_GROUPS, group_planes):
        rows = TILE // dil
        nb64 = rows // HALO
        view = planes.reshape(planes.shape[0], batch, n_tiles, dil, nb64, HALO, LANES)

        def main(s):
            return pl.BlockSpec((None, None, None, dil, nb64, HALO, LANES),
                                lambda b, h, i, s=s: (s * SECTION_PLANES + h, b, i, 0, 0, 0, 0))

        def prev(s, nb64=nb64):
            return pl.BlockSpec((None, None, None, dil, None, HALO, LANES),
                                lambda b, h, i, s=s: (s * SECTION_PLANES + h, b, jnp.maximum(i - 1, 0), 0, nb64 - 1, 0, 0))

        def nxt(s):
            return pl.BlockSpec((None, None, None, dil, None, HALO, LANES),
                                lambda b, h, i, s=s: (s * SECTION_PLANES + h, b, jnp.minimum(i + 1, n_tiles - 1), 0, 0, 0, 0))

        in_specs += [main(0), main(1), prev(1), nxt(1), main(2), prev(2), nxt(2)]
        args += [view] * 7
    gate_view = gate_planes.reshape(gate_planes.shape[0], batch, seq, LANES)
    in_specs.append(pl.BlockSpec((None, None, TILE, LANES), lambda b, h, i: (gate_base + h, b, i, 0)))
    args.append(gate_view)
    max_dil = max(d for _, d in B_GROUPS)
    return pl.pallas_call(
        functools.partial(_attn_b_kernel, seq=seq),
        grid=(batch, N_PAIRS, n_tiles),
        in_specs=in_specs,
        out_specs=pl.BlockSpec((None, TILE, LANES), lambda b, h, i: (b, i, h)),
        out_shape=jax.ShapeDtypeStruct((batch, seq, D_MODEL), BF16),
        scratch_shapes=[
            pltpu.VMEM((max_dil, TILE + 2 * HALO, LANES), BF16),
            pltpu.VMEM((max_dil, TILE + 2 * HALO, LANES), BF16),
            pltpu.VMEM((len(B_GROUPS), TILE, LANES), F32),
            pltpu.VMEM((len(B_GROUPS), TILE, LANES), F32),
            pltpu.VMEM((len(B_GROUPS), TILE, LANES), F32),
            pltpu.VMEM((4, 2 * QBLK, QBLK + 2 * HALO), F32),
        ],
        compiler_params=_cparams(("arbitrary", "arbitrary", "arbitrary")),
    )(*args)


def _out_ln_kernel(y_ref, w_ref, x_ref, g_ref, b_ref, o_ref, *, tm, rc):
    for r0 in range(0, tm, rc):
        rows = slice(r0, r0 + rc)
        f = jnp.dot(y_ref[rows, :], w_ref[...], preferred_element_type=F32)
        z = ALPHA * x_ref[rows, :] + f
        mu = jnp.mean(z, axis=1, keepdims=True)
        zc = z - mu
        var = jnp.mean(zc * zc, axis=1, keepdims=True)
        o_ref[rows, :] = zc * lax.rsqrt(var + LN_EPS) * g_ref[...] + b_ref[...]


def _out_ln(y2d, w, x2d, g, b, *, tm=1024, rc=512):
    n = x2d.shape[0]
    return pl.pallas_call(
        functools.partial(_out_ln_kernel, tm=tm, rc=rc),
        grid=(n // tm,),
        in_specs=[
            pl.BlockSpec((tm, D_MODEL), lambda i: (i, 0)),
            pl.BlockSpec((D_MODEL, D_MODEL), lambda i: (0, 0)),
            pl.BlockSpec((tm, D_MODEL), lambda i: (i, 0)),
            pl.BlockSpec((1, D_MODEL), lambda i: (0, 0)),
            pl.BlockSpec((1, D_MODEL), lambda i: (0, 0)),
        ],
        out_specs=pl.BlockSpec((tm, D_MODEL), lambda i: (i, 0)),
        out_shape=jax.ShapeDtypeStruct((n, D_MODEL), F32),
        compiler_params=_cparams(("arbitrary",)),
    )(y2d, w, x2d, g.reshape(1, D_MODEL), b.reshape(1, D_MODEL))


Q_SCALE_LOG2 = math.log2(math.e) / math.sqrt(HEAD_DIM)


def _scale_q(w):
    return jnp.concatenate([w[:, :D_MODEL] * Q_SCALE_LOG2, w[:, D_MODEL:]], axis=1).astype(BF16)


def _layer_a(x, p, layer_idx):
    batch, seq, _ = x.shape
    x2d = x.reshape(batch * seq, D_MODEL)
    w_in = _scale_q(p["w_in"])
    planes = _project(x2d, w_in, seq=seq, dil=1, tile=1024)
    lam_init = 0.8 - 0.6 * math.exp(-0.3 * layer_idx)
    lamv = jnp.stack([p["lam_q1"], p["lam_k1"], p["lam_q2"], p["lam_k2"]]).astype(F32)
    y = _attn_a(planes, lamv, p["subln_g"].astype(F32).reshape(1, LANES), batch=batch, seq=seq, lam_init=lam_init)
    out = _out_ln(y.reshape(batch * seq, D_MODEL), p["w_out"].astype(BF16), x2d, p["ln_g"], p["ln_b"])
    return out.reshape(batch, seq, D_MODEL)


def _layer_b(x, p):
    batch, seq, _ = x.shape
    x2d = x.reshape(batch * seq, D_MODEL)
    w_in = p["w_in"]
    group_planes = []
    gate_planes = None
    for gi, (_, dil) in enumerate(B_GROUPS):
        w = w_in[:, 3 * gi * D_MODEL:3 * (gi + 1) * D_MODEL]
        if dil == 1:
            w = jnp.concatenate([w, w_in[:, 3 * len(B_GROUPS) * D_MODEL:]], axis=1)
        planes = _project(x2d, _scale_q(w), seq=seq, dil=dil, tile=TILE)
        if dil == 1:
            gate_planes = planes
        group_planes.append(planes)
    y = _attn_b(group_planes, gate_planes, 3 * SECTION_PLANES, batch=batch, seq=seq)
    out = _out_ln(y.reshape(batch * seq, D_MODEL), p["w_out"].astype(BF16), x2d, p["ln_g"], p["ln_b"])
    return out.reshape(batch, seq, D_MODEL)


def _trunk(x, layers):
    for i, p in enumerate(layers):
        x = _layer_a(x, p, i) if i % 2 == 0 else _layer_b(x, p)
    return x


def kernel(x_prompt, x_sample, w_in_0, lam_q1_0, lam_k1_0, lam_q2_0, lam_k2_0, subln_g_0, w_out_0, ln_g_0, ln_b_0, w_in_1, w_out_1, ln_g_1, ln_b_1, w_in_2, lam_q1_2, lam_k1_2, lam_q2_2, lam_k2_2, subln_g_2, w_out_2, ln_g_2, ln_b_2, w_in_3, w_out_3, ln_g_3, ln_b_3):
    layers = [
        dict(w_in=w_in_0, lam_q1=lam_q1_0, lam_k1=lam_k1_0, lam_q2=lam_q2_0, lam_k2=lam_k2_0,
             subln_g=subln_g_0, w_out=w_out_0, ln_g=ln_g_0, ln_b=ln_b_0),
        dict(w_in=w_in_1, w_out=w_out_1, ln_g=ln_g_1, ln_b=ln_b_1),
        dict(w_in=w_in_2, lam_q1=lam_q1_2, lam_k1=lam_k1_2, lam_q2=lam_q2_2, lam_k2=lam_k2_2,
             subln_g=subln_g_2, w_out=w_out_2, ln_g=ln_g_2, ln_b=ln_b_2),
        dict(w_in=w_in_3, w_out=w_out_3, ln_g=ln_g_3, ln_b=ln_b_3),
    ]
    return (_trunk(x_prompt, layers), _trunk(x_sample, layers))
```

```python
import functools
import math

import jax
import jax.numpy as jnp
from jax import lax
from jax.experimental import pallas as pl
from jax.experimental.pallas import tpu as pltpu

F32 = jnp.float32
BF16 = jnp.bfloat16

D_MODEL = 1024
DEPTH = 4
ROPE_THETA = 500000.0
LN_EPS = 1e-5
ALPHA = (2.0 * DEPTH) ** 0.25
HEAD_DIM = 64
LANES = 128
N_PAIRS = D_MODEL // LANES
SECTION_PLANES = D_MODEL // LANES
B_GROUPS = ((128, 1), (512, 4), (2048, 16))
RADIUS = 64
TILE = 2048
HALO = 64
QBLK = 128
MASKED = -1e30
GROUP = 4
VMEM_LIMIT = 56 * 1024 * 1024


def _cparams(sem, flags=None):
    return pltpu.CompilerParams(dimension_semantics=sem, vmem_limit_bytes=VMEM_LIMIT, flags=flags)


def _rope_tables(seq, dil, tm):
    half = HEAD_DIM // 8
    pos = jnp.arange(seq, dtype=jnp.int32).reshape(seq // tm, tm // dil, dil).transpose(0, 2, 1).reshape(seq)
    dim = jnp.arange(LANES, dtype=jnp.int32) % HEAD_DIM
    rotated = dim < 2 * half
    inv = jnp.where(rotated, ROPE_THETA ** (-(dim % half).astype(F32) / half), 0.0)
    ang = pos.astype(F32)[:, None] * inv[None, :]
    sin = jnp.sin(ang)
    c = jnp.cos(ang)
    sa = sin * jnp.where(dim < half, -1.0, 0.0)[None, :]
    sb = sin * jnp.where(rotated & (dim >= half), 1.0, 0.0)[None, :]
    return c, sa, sb


X_SLABS = D_MODEL // LANES


ROPE_COLS = 2 * D_MODEL


def _proj_kernel(*refs, dil, tm, sub_n, n_out):
    x_refs = refs[:X_SLABS]
    w_ref, c_ref, sa_ref, sb_ref, o_ref, xs_ref = refs[X_SLABS:]
    res_rows = tm // dil

    for c in range(X_SLABS):
        for r in range(dil):
            src = x_refs[c][pl.ds(r, res_rows, stride=dil), :] if dil > 1 else x_refs[c][...]
            xs_ref[r * res_rows:(r + 1) * res_rows, c * LANES:(c + 1) * LANES] = src.astype(BF16)

    for sd in range(n_out // sub_n):
        acc = jnp.dot(xs_ref[...], w_ref[:, sd * sub_n:(sd + 1) * sub_n], preferred_element_type=F32)
        for t in range(sub_n // LANES):
            a = acc[:, t * LANES:(t + 1) * LANES]
            if sd * sub_n < ROPE_COLS:
                a = a * c_ref[...] + pltpu.roll(a, LANES - 8, 1) * sa_ref[...] + pltpu.roll(a, 8, 1) * sb_ref[...]
            ab = a.astype(BF16)
            for r in range(dil):
                o_ref[sd * (sub_n // LANES) + t, r] = ab[r * res_rows:(r + 1) * res_rows]


def _project(x2d, w, *, seq, dil, tile, tm=1024, sub_n=512):
    n = x2d.shape[0]
    n_out = w.shape[1]
    per_tile = tile // tm
    n_seq_tiles = seq // tm
    tabs = _rope_tables(seq, dil, tm)
    kern = functools.partial(_proj_kernel, dil=dil, tm=tm, sub_n=sub_n, n_out=n_out)
    tab_spec = pl.BlockSpec((tm, LANES), lambda i: (i % n_seq_tiles, 0))
    x_specs = [pl.BlockSpec((tm, LANES), lambda i, c=c: (i, c)) for c in range(X_SLABS)]
    return pl.pallas_call(
        kern,
        grid=(n // tm,),
        in_specs=x_specs + [
            pl.BlockSpec((D_MODEL, n_out), lambda i: (0, 0)),
            tab_spec, tab_spec, tab_spec,
        ],
        out_specs=pl.BlockSpec((n_out // LANES, None, dil, tm // dil, LANES),
                               lambda i: (0, i // per_tile, 0, i % per_tile, 0)),
        out_shape=jax.ShapeDtypeStruct((n_out // LANES, n // tile, dil, tile // dil, LANES), BF16),
        scratch_shapes=[pltpu.VMEM((tm, D_MODEL), BF16)],
        compiler_params=_cparams(("arbitrary",)),
    )(*([x2d] * X_SLABS), w, *tabs)


N_SCORE_BUFS = 2


def _silu(g):
    return g * (1.0 / (1.0 + jnp.exp(-g)))


def _attn_a_kernel(lamv_ref, subg_ref, q_ref, k_ref, v_ref, g_ref, o_ref,
                   qt_ref, vt_ref, s0_ref, s1_ref, m_ref, l_ref, acc_ref, *, lam_init, bq, bk, nkv):
    @pl.when(pl.program_id(2) == 0)
    def _():
        for j in range(nkv):
            vt_ref[:, j * bk:(j + 1) * bk] = v_ref[j * bk:(j + 1) * bk, :].astype(F32).T.astype(BF16)

    qt = q_ref[...].astype(F32).T
    dim = lax.broadcasted_iota(jnp.int32, qt.shape, 0)
    qt_ref[:, 0:bq] = jnp.where(dim < HEAD_DIM, qt, 0.0).astype(BF16)
    qt_ref[:, bq:2 * bq] = jnp.where(dim >= HEAD_DIM, qt, 0.0).astype(BF16)
    m_ref[...] = jnp.full(m_ref.shape, -jnp.inf, F32)
    l_ref[...] = jnp.zeros(l_ref.shape, F32)
    acc_ref[...] = jnp.zeros(acc_ref.shape, F32)

    def scores(j, s_ref):
        kc = k_ref[pl.ds(pl.multiple_of(j * bk, bk), bk), :]
        s_ref[...] = jnp.dot(kc, qt_ref[...], preferred_element_type=F32)

    def update(j, s_ref):
        m_prev = m_ref[...]
        m_new = jnp.maximum(m_prev, jnp.max(s_ref[...], axis=0, keepdims=True))
        alpha = jnp.exp2(m_prev - m_new)
        p = jnp.exp2(s_ref[...] - m_new)
        l_ref[...] = alpha * l_ref[...] + jnp.sum(p, axis=0, keepdims=True)
        pv = jnp.dot(vt_ref[:, j * bk:(j + 1) * bk], p.astype(BF16), preferred_element_type=F32)
        acc_ref[...] = alpha * acc_ref[...] + pv
        m_ref[...] = m_new

    bufs = (s0_ref, s1_ref)
    scores(0, bufs[0])
    for j in range(nkv - 1):
        scores(j + 1, bufs[(j + 1) % N_SCORE_BUFS])
        update(j, bufs[j % N_SCORE_BUFS])
    update(nkv - 1, bufs[(nkv - 1) % N_SCORE_BUFS])

    lv = lamv_ref[...]
    lam = (jnp.exp(jnp.sum(lv[0:1] * lv[1:2], axis=1, keepdims=True))
           - jnp.exp(jnp.sum(lv[2:3] * lv[3:4], axis=1, keepdims=True)) + lam_init)
    ot = acc_ref[...] / l_ref[...]
    odt = ot[:, 0:bq] - lam * ot[:, bq:2 * bq]
    odt = odt * lax.rsqrt(jnp.mean(odt * odt, axis=0, keepdims=True) + LN_EPS)
    od = odt.T * subg_ref[...] * (1.0 - lam_init)
    o_ref[...] = (od * _silu(g_ref[...].astype(F32))).astype(BF16)


def _attn_a(planes, lamv, subg, *, batch, seq, lam_init, bq=512, bk=1024):
    pl4 = planes.reshape(4 * SECTION_PLANES, batch, seq, LANES)
    nkv = seq // bk
    assert seq % bk == 0 and seq % bq == 0
    kern = functools.partial(_attn_a_kernel, lam_init=lam_init, bq=bq, bk=bk, nkv=nkv)

    def q_rows(s):
        return pl.BlockSpec((None, None, bq, LANES), lambda b, h, qi: (s * SECTION_PLANES + h, b, qi, 0))

    def all_rows(s):
        return pl.BlockSpec((None, None, seq, LANES), lambda b, h, qi: (s * SECTION_PLANES + h, b, 0, 0))

    return pl.pallas_call(
        kern,
        grid=(batch, N_PAIRS, seq // bq),
        in_specs=[
            pl.BlockSpec((4, HEAD_DIM), lambda b, h, qi: (0, 0)),
            pl.BlockSpec((1, LANES), lambda b, h, qi: (0, 0)),
            q_rows(0), all_rows(1), all_rows(2), q_rows(3),
        ],
        out_specs=pl.BlockSpec((None, bq, LANES), lambda b, h, qi: (b, qi, h)),
        out_shape=jax.ShapeDtypeStruct((batch, seq, D_MODEL), BF16),
        scratch_shapes=[
            pltpu.VMEM((LANES, 2 * bq), BF16),
            pltpu.VMEM((LANES, seq), BF16),
        ] + [pltpu.VMEM((bk, 2 * bq), F32)] * N_SCORE_BUFS + [
            pltpu.VMEM((1, 2 * bq), F32),
            pltpu.VMEM((1, 2 * bq), F32),
            pltpu.VMEM((LANES, 2 * bq), F32),
        ],
        compiler_params=_cparams(("arbitrary", "arbitrary", "arbitrary")),
    )(lamv, subg, pl4, pl4, pl4, pl4)


def _attn_b_group(gi, dil, seq, q_ref, k_ref, kp_ref, kn_ref, v_ref, vp_ref, vn_ref,
                  kw_ref, vw_ref, og_ref, mg_ref, lg_ref, bias_ref):
    rows = TILE // dil
    nb64 = rows // HALO
    tile_i = pl.program_id(2)
    length = seq // dil

    kw_ref[0:dil, 0:HALO, :] = kp_ref[...]
    vw_ref[0:dil, 0:HALO, :] = vp_ref[...]
    for n in range(nb64):
        kw_ref[0:dil, HALO + n * HALO:2 * HALO + n * HALO, :] = k_ref[:, n]
        vw_ref[0:dil, HALO + n * HALO:2 * HALO + n * HALO, :] = v_ref[:, n]
    kw_ref[0:dil, HALO + rows:2 * HALO + rows, :] = kn_ref[...]
    vw_ref[0:dil, HALO + rows:2 * HALO + rows, :] = vn_ref[...]

    win = QBLK + 2 * HALO
    lane_q = lax.broadcasted_iota(jnp.int32, (QBLK, LANES), 1)
    lo_q = lane_q < HEAD_DIM
    blocks = rows // QBLK
    unroll = max(1, 8 // blocks)

    def block_scores(r, blk):
        q2 = q_ref[r, 2 * blk:2 * blk + 2].reshape(QBLK, LANES)
        zero = jnp.zeros_like(q2)
        qs = jnp.concatenate([jnp.where(lo_q, q2, zero), jnp.where(lo_q, zero, q2)], axis=0)
        kwin = kw_ref[r, blk * QBLK:blk * QBLK + win, :]
        s = lax.dot_general(qs, kwin, (((1,), (1,)), ((), ())), preferred_element_type=F32)
        first = tile_i * rows + blk * QBLK
        edge = (first == 0).astype(jnp.int32) + 2 * (first + QBLK == length).astype(jnp.int32)
        return s + bias_ref[edge]

    def block_softmax(s):
        m = jnp.max(s, axis=1, keepdims=True)
        p = jnp.exp2(s - m)
        return p.astype(BF16), m, jnp.sum(p, axis=1, keepdims=True)

    def per_head(a):
        return jnp.where(lo_q, jnp.broadcast_to(a[0:QBLK], (QBLK, LANES)),
                         jnp.broadcast_to(a[QBLK:2 * QBLK], (QBLK, LANES)))

    def block_output(r, blk, pb, m, l):
        vwin = vw_ref[r, blk * QBLK:blk * QBLK + win, :]
        o0 = jnp.dot(pb[0:QBLK], vwin, preferred_element_type=F32)
        o1 = jnp.dot(pb[QBLK:2 * QBLK], vwin, preferred_element_type=F32)
        tok = pl.ds(r + blk * QBLK * dil, QBLK, stride=dil) if dil > 1 else pl.ds(blk * QBLK, QBLK)
        og_ref[gi, tok, :] = jnp.where(lo_q, o0, o1)
        mg_ref[gi, tok, :] = per_head(m)
        lg_ref[gi, tok, :] = per_head(l)

    def body(it, carry):
        items = [(it * unroll + u, blk) for u in range(unroll) for blk in range(blocks)]
        for g0 in range(0, len(items), GROUP):
            group = items[g0:g0 + GROUP]
            scores = [block_scores(r, blk) for r, blk in group]
            soft = [block_softmax(s) for s in scores]
            for (r, blk), (pb, m, l) in zip(group, soft):
                block_output(r, blk, pb, m, l)
        return carry

    if dil == 1:
        body(0, 0)
    else:
        lax.fori_loop(0, dil // unroll, body, 0)


def _attn_b_kernel(*refs, seq):
    n_in = 7 * len(B_GROUPS) + 1
    gate_ref = refs[n_in - 1]
    o_ref = refs[n_in]
    kw_ref, vw_ref, og_ref, mg_ref, lg_ref, bias_ref = refs[n_in + 1:]

    @pl.when((pl.program_id(0) == 0) & (pl.program_id(1) == 0) & (pl.program_id(2) == 0))
    def _():
        win = QBLK + 2 * HALO
        row_i = lax.broadcasted_iota(jnp.int32, (2 * QBLK, win), 0)
        col_i = lax.broadcasted_iota(jnp.int32, (2 * QBLK, win), 1)
        qpos = jnp.where(row_i >= QBLK, row_i - QBLK, row_i)
        band = jnp.where(jnp.abs(col_i - HALO - qpos) <= RADIUS, 0.0, MASKED)
        low = jnp.where(col_i < HALO, MASKED, 0.0)
        high = jnp.where(col_i >= QBLK + HALO, MASKED, 0.0)
        bias_ref[0] = band
        bias_ref[1] = band + low
        bias_ref[2] = band + high
        bias_ref[3] = band + low + high

    for gi, (_, dil) in enumerate(B_GROUPS):
        _attn_b_group(gi, dil, seq, *refs[7 * gi:7 * gi + 7], kw_ref, vw_ref, og_ref, mg_ref, lg_ref, bias_ref)
    m0, m1, m2 = mg_ref[0], mg_ref[1], mg_ref[2]
    mx = jnp.maximum(jnp.maximum(m0, m1), m2)
    e0, e1, e2 = jnp.exp2(m0 - mx), jnp.exp2(m1 - mx), jnp.exp2(m2 - mx)
    den = e0 * lg_ref[0] + e1 * lg_ref[1] + e2 * lg_ref[2]
    o = (e0 * og_ref[0] + e1 * og_ref[1] + e2 * og_ref[2]) / den
    o_ref[...] = (o * _silu(gate_ref[...].astype(F32))).astype(BF16)


def _attn_b(group_planes, gate_planes, gate_base, *, batch, seq):
    n_tiles = seq // TILE
    in_specs, args = [], []
    for (_, dil), planes in zip(B_GROUPS, group_planes):
        rows = TILE // dil
        nb64 = rows // HALO
        view = planes.reshape(planes.shape[0], batch, n_tiles, dil, nb64, HALO, LANES)

        def main(s):
            return pl.BlockSpec((None, None, None, dil, nb64, HALO, LANES),
                                lambda b, h, i, s=s: (s * SECTION_PLANES + h, b, i, 0, 0, 0, 0))

        def prev(s, nb64=nb64):
            return pl.BlockSpec((None, None, None, dil, None, HALO, LANES),
                                lambda b, h, i, s=s: (s * SECTION_PLANES + h, b, jnp.maximum(i - 1, 0), 0, nb64 - 1, 0, 0))

        def nxt(s):
            return pl.BlockSpec((None, None, None, dil, None, HALO, LANES),
                                lambda b, h, i, s=s: (s * SECTION_PLANES + h, b, jnp.minimum(i + 1, n_tiles - 1), 0, 0, 0, 0))

        in_specs += [main(0), main(1), prev(1), nxt(1), main(2), prev(2), nxt(2)]
        args += [view] * 7
    gate_view = gate_planes.reshape(gate_planes.shape[0], batch, seq, LANES)
    in_specs.append(pl.BlockSpec((None, None, TILE, LANES), lambda b, h, i: (gate_base + h, b, i, 0)))
    args.append(gate_view)
    max_dil = max(d for _, d in B_GROUPS)
    return pl.pallas_call(
        functools.partial(_attn_b_kernel, seq=seq),
        grid=(batch, N_PAIRS, n_tiles),
        in_specs=in_specs,
        out_specs=pl.BlockSpec((None, TILE, LANES), lambda b, h, i: (b, i, h)),
        out_shape=jax.ShapeDtypeStruct((batch, seq, D_MODEL), BF16),
        scratch_shapes=[
            pltpu.VMEM((max_dil, TILE + 2 * HALO, LANES), BF16),
            pltpu.VMEM((max_dil, TILE + 2 * HALO, LANES), BF16),
            pltpu.VMEM((len(B_GROUPS), TILE, LANES), F32),
            pltpu.VMEM((len(B_GROUPS), TILE, LANES), F32),
            pltpu.VMEM((len(B_GROUPS), TILE, LANES), F32),
            pltpu.VMEM((4, 2 * QBLK, QBLK + 2 * HALO), F32),
        ],
        compiler_params=_cparams(("arbitrary", "arbitrary", "arbitrary")),
    )(*args)


def _out_ln_kernel(y_ref, w_ref, x_ref, g_ref, b_ref, o_ref, *, tm, rc):
    for r0 in range(0, tm, rc):
        rows = slice(r0, r0 + rc)
        f = jnp.dot(y_ref[rows, :], w_ref[...], preferred_element_type=F32)
        z = ALPHA * x_ref[rows, :] + f
        mu = jnp.mean(z, axis=1, keepdims=True)
        zc = z - mu
        var = jnp.mean(zc * zc, axis=1, keepdims=True)
        o_ref[rows, :] = zc * lax.rsqrt(var + LN_EPS) * g_ref[...] + b_ref[...]


def _out_ln(y2d, w, x2d, g, b, *, tm=1024, rc=512):
    n = x2d.shape[0]
    return pl.pallas_call(
        functools.partial(_out_ln_kernel, tm=tm, rc=rc),
        grid=(n // tm,),
        in_specs=[
            pl.BlockSpec((tm, D_MODEL), lambda i: (i, 0)),
            pl.BlockSpec((D_MODEL, D_MODEL), lambda i: (0, 0)),
            pl.BlockSpec((tm, D_MODEL), lambda i: (i, 0)),
            pl.BlockSpec((1, D_MODEL), lambda i: (0, 0)),
            pl.BlockSpec((1, D_MODEL), lambda i: (0, 0)),
        ],
        out_specs=pl.BlockSpec((tm, D_MODEL), lambda i: (i, 0)),
        out_shape=jax.ShapeDtypeStruct((n, D_MODEL), F32),
        compiler_params=_cparams(("arbitrary",)),
    )(y2d, w, x2d, g.reshape(1, D_MODEL), b.reshape(1, D_MODEL))


Q_SCALE_LOG2 = math.log2(math.e) / math.sqrt(HEAD_DIM)


def _scale_q(w):
    return jnp.concatenate([w[:, :D_MODEL] * Q_SCALE_LOG2, w[:, D_MODEL:]], axis=1).astype(BF16)


def _layer_a(x, p, layer_idx):
    batch, seq, _ = x.shape
    x2d = x.reshape(batch * seq, D_MODEL)
    w_in = _scale_q(p["w_in"])
    planes = _project(x2d, w_in, seq=seq, dil=1, tile=1024)
    lam_init = 0.8 - 0.6 * math.exp(-0.3 * layer_idx)
    lamv = jnp.stack([p["lam_q1"], p["lam_k1"], p["lam_q2"], p["lam_k2"]]).astype(F32)
    y = _attn_a(planes, lamv, p["subln_g"].astype(F32).reshape(1, LANES), batch=batch, seq=seq, lam_init=lam_init)
    out = _out_ln(y.reshape(batch * seq, D_MODEL), p["w_out"].astype(BF16), x2d, p["ln_g"], p["ln_b"])
    return out.reshape(batch, seq, D_MODEL)


def _layer_b(x, p):
    batch, seq, _ = x.shape
    x2d = x.reshape(batch * seq, D_MODEL)
    w_in = p["w_in"]
    group_planes = []
    gate_planes = None
    for gi, (_, dil) in enumerate(B_GROUPS):
        w = w_in[:, 3 * gi * D_MODEL:3 * (gi + 1) * D_MODEL]
        if dil == 1:
            w = jnp.concatenate([w, w_in[:, 3 * len(B_GROUPS) * D_MODEL:]], axis=1)
        planes = _project(x2d, _scale_q(w), seq=seq, dil=dil, tile=TILE)
        if dil == 1:
            gate_planes = planes
        group_planes.append(planes)
    y = _attn_b(group_planes, gate_planes, 3 * SECTION_PLANES, batch=batch, seq=seq)
    out = _out_ln(y.reshape(batch * seq, D_MODEL), p["w_out"].astype(BF16), x2d, p["ln_g"], p["ln_b"])
    return out.reshape(batch, seq, D_MODEL)


def _trunk(x, layers):
    for i, p in enumerate(layers):
        x = _layer_a(x, p, i) if i % 2 == 0 else _layer_b(x, p)
    return x


def kernel(x_prompt, x_sample, w_in_0, lam_q1_0, lam_k1_0, lam_q2_0, lam_k2_0, subln_g_0, w_out_0, ln_g_0, ln_b_0, w_in_1, w_out_1, ln_g_1, ln_b_1, w_in_2, lam_q1_2, lam_k1_2, lam_q2_2, lam_k2_2, subln_g_2, w_out_2, ln_g_2, ln_b_2, w_in_3, w_out_3, ln_g_3, ln_b_3):
    layers = [
        dict(w_in=w_in_0, lam_q1=lam_q1_0, lam_k1=lam_k1_0, lam_q2=lam_q2_0, lam_k2=lam_k2_0,
             subln_g=subln_g_0, w_out=w_out_0, ln_g=ln_g_0, ln_b=ln_b_0),
        dict(w_in=w_in_1, w_out=w_out_1, ln_g=ln_g_1, ln_b=ln_b_1),
        dict(w_in=w_in_2, lam_q1=lam_q1_2, lam_k1=lam_k1_2, lam_q2=lam_q2_2, lam_k2=lam_k2_2,
             subln_g=subln_g_2, w_out=w_out_2, ln_g=ln_g_2, ln_b=ln_b_2),
        dict(w_in=w_in_3, w_out=w_out_3, ln_g=ln_g_3, ln_b=ln_b_3),
    ]
    return (_trunk(x_prompt, layers), _trunk(x_sample, layers))
```

```python
import functools
import math

import jax
import jax.numpy as jnp
from jax import lax
from jax.experimental import pallas as pl
from jax.experimental.pallas import tpu as pltpu

F32 = jnp.float32
BF16 = jnp.bfloat16

D_MODEL = 1024
DEPTH = 4
ROPE_THETA = 500000.0
LN_EPS = 1e-5
ALPHA = (2.0 * DEPTH) ** 0.25
HEAD_DIM = 64
LANES = 128
N_PAIRS = D_MODEL // LANES
SECTION_PLANES = D_MODEL // LANES
B_GROUPS = ((128, 1), (512, 4), (2048, 16))
RADIUS = 64
TILE = 2048
HALO = 64
QBLK = 128
MASKED = -1e30
GROUP = 4
VMEM_LIMIT = 56 * 1024 * 1024


def _cparams(sem, flags=None):
    return pltpu.CompilerParams(dimension_semantics=sem, vmem_limit_bytes=VMEM_LIMIT, flags=flags)


def _rope_tables(seq, dil, tm):
    half = HEAD_DIM // 8
    pos = jnp.arange(seq, dtype=jnp.int32).reshape(seq // tm, tm // dil, dil).transpose(0, 2, 1).reshape(seq)
    dim = jnp.arange(LANES, dtype=jnp.int32) % HEAD_DIM
    rotated = dim < 2 * half
    inv = jnp.where(rotated, ROPE_THETA ** (-(dim % half).astype(F32) / half), 0.0)
    ang = pos.astype(F32)[:, None] * inv[None, :]
    sin = jnp.sin(ang)
    c = jnp.cos(ang)
    sa = sin * jnp.where(dim < half, -1.0, 0.0)[None, :]
    sb = sin * jnp.where(rotated & (dim >= half), 1.0, 0.0)[None, :]
    return c, sa, sb


X_SLABS = D_MODEL // LANES


ROPE_COLS = 2 * D_MODEL


def _proj_kernel(*refs, dil, tm, sub_n, n_out):
    x_refs = refs[:X_SLABS]
    w_ref, c_ref, sa_ref, sb_ref, o_ref, xs_ref = refs[X_SLABS:]
    res_rows = tm // dil

    for c in range(X_SLABS):
        for r in range(dil):
            src = x_refs[c][pl.ds(r, res_rows, stride=dil), :] if dil > 1 else x_refs[c][...]
            xs_ref[r * res_rows:(r + 1) * res_rows, c * LANES:(c + 1) * LANES] = src.astype(BF16)

    for sd in range(n_out // sub_n):
        acc = jnp.dot(xs_ref[...], w_ref[:, sd * sub_n:(sd + 1) * sub_n], preferred_element_type=F32)
        for t in range(sub_n // LANES):
            a = acc[:, t * LANES:(t + 1) * LANES]
            if sd * sub_n < ROPE_COLS:
                a = a * c_ref[...] + pltpu.roll(a, LANES - 8, 1) * sa_ref[...] + pltpu.roll(a, 8, 1) * sb_ref[...]
            ab = a.astype(BF16)
            for r in range(dil):
                o_ref[sd * (sub_n // LANES) + t, r] = ab[r * res_rows:(r + 1) * res_rows]


def _project(x2d, w, *, seq, dil, tile, tm=1024, sub_n=512):
    n = x2d.shape[0]
    n_out = w.shape[1]
    per_tile = tile // tm
    n_seq_tiles = seq // tm
    tabs = _rope_tables(seq, dil, tm)
    kern = functools.partial(_proj_kernel, dil=dil, tm=tm, sub_n=sub_n, n_out=n_out)
    tab_spec = pl.BlockSpec((tm, LANES), lambda i: (i % n_seq_tiles, 0))
    x_specs = [pl.BlockSpec((tm, LANES), lambda i, c=c: (i, c)) for c in range(X_SLABS)]
    return pl.pallas_call(
        kern,
        grid=(n // tm,),
        in_specs=x_specs + [
            pl.BlockSpec((D_MODEL, n_out), lambda i: (0, 0)),
            tab_spec, tab_spec, tab_spec,
        ],
        out_specs=pl.BlockSpec((n_out // LANES, None, dil, tm // dil, LANES),
                               lambda i: (0, i // per_tile, 0, i % per_tile, 0)),
        out_shape=jax.ShapeDtypeStruct((n_out // LANES, n // tile, dil, tile // dil, LANES), BF16),
        scratch_shapes=[pltpu.VMEM((tm, D_MODEL), BF16)],
        compiler_params=_cparams(("arbitrary",)),
    )(*([x2d] * X_SLABS), w, *tabs)


N_SCORE_BUFS = 2


def _silu(g):
    return g * (1.0 / (1.0 + jnp.exp(-g)))


def _attn_a_kernel(lamv_ref, subg_ref, q_ref, k_ref, v_ref, g_ref, o_ref,
                   qt_ref, vt_ref, s0_ref, s1_ref, m_ref, l_ref, acc_ref, *, lam_init, bq, bk, nkv):
    @pl.when(pl.program_id(2) == 0)
    def _():
        for j in range(nkv):
            vt_ref[j] = v_ref[j * bk:(j + 1) * bk, :].astype(F32).T.astype(BF16)

    qt = q_ref[...].astype(F32).T
    dim = lax.broadcasted_iota(jnp.int32, qt.shape, 0)
    qt_ref[:, 0:bq] = jnp.where(dim < HEAD_DIM, qt, 0.0).astype(BF16)
    qt_ref[:, bq:2 * bq] = jnp.where(dim >= HEAD_DIM, qt, 0.0).astype(BF16)
    m_ref[...] = jnp.full(m_ref.shape, -jnp.inf, F32)
    l_ref[...] = jnp.zeros(l_ref.shape, F32)
    acc_ref[...] = jnp.zeros(acc_ref.shape, F32)

    def scores(j, s_ref):
        kc = k_ref[pl.ds(pl.multiple_of(j * bk, bk), bk), :]
        s_ref[...] = jnp.dot(kc, qt_ref[...], preferred_element_type=F32)

    def update(j, s_ref):
        m_prev = m_ref[...]
        m_new = jnp.maximum(m_prev, jnp.max(s_ref[...], axis=0, keepdims=True))
        alpha = jnp.exp2(m_prev - m_new)
        p = jnp.exp2(s_ref[...] - m_new)
        l_ref[...] = alpha * l_ref[...] + jnp.sum(p, axis=0, keepdims=True)
        pv = jnp.dot(vt_ref[j], p.astype(BF16), preferred_element_type=F32)
        acc_ref[...] = alpha * acc_ref[...] + pv
        m_ref[...] = m_new

    scores(0, s0_ref)

    def pair(i, carry):
        scores(2 * i + 1, s1_ref)
        update(2 * i, s0_ref)
        scores(2 * i + 2, s0_ref)
        update(2 * i + 1, s1_ref)
        return carry

    lax.fori_loop(0, (nkv - 2) // 2, pair, 0)
    scores(nkv - 1, s1_ref)
    update(nkv - 2, s0_ref)
    update(nkv - 1, s1_ref)

    lv = lamv_ref[...]
    lam = (jnp.exp(jnp.sum(lv[0:1] * lv[1:2], axis=1, keepdims=True))
           - jnp.exp(jnp.sum(lv[2:3] * lv[3:4], axis=1, keepdims=True)) + lam_init)
    ot = acc_ref[...] / l_ref[...]
    odt = ot[:, 0:bq] - lam * ot[:, bq:2 * bq]
    odt = odt * lax.rsqrt(jnp.mean(odt * odt, axis=0, keepdims=True) + LN_EPS)
    od = odt.T * subg_ref[...] * (1.0 - lam_init)
    o_ref[...] = (od * _silu(g_ref[...].astype(F32))).astype(BF16)


def _attn_a(planes, lamv, subg, *, batch, seq, lam_init, bq=512, bk=1024):
    pl4 = planes.reshape(4 * SECTION_PLANES, batch, seq, LANES)
    nkv = seq // bk
    assert seq % bk == 0 and seq % bq == 0
    kern = functools.partial(_attn_a_kernel, lam_init=lam_init, bq=bq, bk=bk, nkv=nkv)

    def q_rows(s):
        return pl.BlockSpec((None, None, bq, LANES), lambda b, h, qi: (s * SECTION_PLANES + h, b, qi, 0))

    def all_rows(s):
        return pl.BlockSpec((None, None, seq, LANES), lambda b, h, qi: (s * SECTION_PLANES + h, b, 0, 0))

    return pl.pallas_call(
        kern,
        grid=(batch, N_PAIRS, seq // bq),
        in_specs=[
            pl.BlockSpec((4, HEAD_DIM), lambda b, h, qi: (0, 0)),
            pl.BlockSpec((1, LANES), lambda b, h, qi: (0, 0)),
            q_rows(0), all_rows(1), all_rows(2), q_rows(3),
        ],
        out_specs=pl.BlockSpec((None, bq, LANES), lambda b, h, qi: (b, qi, h)),
        out_shape=jax.ShapeDtypeStruct((batch, seq, D_MODEL), BF16),
        scratch_shapes=[
            pltpu.VMEM((LANES, 2 * bq), BF16),
            pltpu.VMEM((nkv, LANES, bk), BF16),
        ] + [pltpu.VMEM((bk, 2 * bq), F32)] * N_SCORE_BUFS + [
            pltpu.VMEM((1, 2 * bq), F32),
            pltpu.VMEM((1, 2 * bq), F32),
            pltpu.VMEM((LANES, 2 * bq), F32),
        ],
        compiler_params=_cparams(("arbitrary", "arbitrary", "arbitrary")),
    )(lamv, subg, pl4, pl4, pl4, pl4)


def _attn_b_group(gi, dil, seq, q_ref, k_ref, kp_ref, kn_ref, v_ref, vp_ref, vn_ref,
                  kw_ref, vw_ref, og_ref, mg_ref, lg_ref, bias_ref):
    rows = TILE // dil
    nb64 = rows // HALO
    tile_i = pl.program_id(2)
    length = seq // dil

    kw_ref[0:dil, 0:HALO, :] = kp_ref[...]
    vw_ref[0:dil, 0:HALO, :] = vp_ref[...]
    for n in range(nb64):
        kw_ref[0:dil, HALO + n * HALO:2 * HALO + n * HALO, :] = k_ref[:, n]
        vw_ref[0:dil, HALO + n * HALO:2 * HALO + n * HALO, :] = v_ref[:, n]
    kw_ref[0:dil, HALO + rows:2 * HALO + rows, :] = kn_ref[...]
    vw_ref[0:dil, HALO + rows:2 * HALO + rows, :] = vn_ref[...]

    win = QBLK + 2 * HALO
    lane_q = lax.broadcasted_iota(jnp.int32, (QBLK, LANES), 1)
    lo_q = lane_q < HEAD_DIM
    blocks = rows // QBLK
    unroll = max(1, 8 // blocks)

    def block_scores(r, blk):
        q2 = q_ref[r, 2 * blk:2 * blk + 2].reshape(QBLK, LANES)
        zero = jnp.zeros_like(q2)
        qs = jnp.concatenate([jnp.where(lo_q, q2, zero), jnp.where(lo_q, zero, q2)], axis=0)
        kwin = kw_ref[r, blk * QBLK:blk * QBLK + win, :]
        s = lax.dot_general(qs, kwin, (((1,), (1,)), ((), ())), preferred_element_type=F32)
        first = tile_i * rows + blk * QBLK
        edge = (first == 0).astype(jnp.int32) + 2 * (first + QBLK == length).astype(jnp.int32)
        return s + bias_ref[edge]

    def block_softmax(s):
        m = jnp.max(s, axis=1, keepdims=True)
        p = jnp.exp2(s - m)
        return p.astype(BF16), m, jnp.sum(p, axis=1, keepdims=True)

    def per_head(a):
        return jnp.where(lo_q, jnp.broadcast_to(a[0:QBLK], (QBLK, LANES)),
                         jnp.broadcast_to(a[QBLK:2 * QBLK], (QBLK, LANES)))

    def block_output(r, blk, pb, m, l):
        vwin = vw_ref[r, blk * QBLK:blk * QBLK + win, :]
        o0 = jnp.dot(pb[0:QBLK], vwin, preferred_element_type=F32)
        o1 = jnp.dot(pb[QBLK:2 * QBLK], vwin, preferred_element_type=F32)
        tok = pl.ds(r + blk * QBLK * dil, QBLK, stride=dil) if dil > 1 else pl.ds(blk * QBLK, QBLK)
        og_ref[gi, tok, :] = jnp.where(lo_q, o0, o1)
        mg_ref[gi, tok, :] = per_head(m)
        lg_ref[gi, tok, :] = per_head(l)

    def body(it, carry):
        items = [(it * unroll + u, blk) for u in range(unroll) for blk in range(blocks)]
        for g0 in range(0, len(items), GROUP):
            group = items[g0:g0 + GROUP]
            scores = [block_scores(r, blk) for r, blk in group]
            soft = [block_softmax(s) for s in scores]
            for (r, blk), (pb, m, l) in zip(group, soft):
                block_output(r, blk, pb, m, l)
        return carry

    if dil == 1:
        body(0, 0)
    else:
        lax.fori_loop(0, dil // unroll, body, 0)


def _attn_b_kernel(*refs, seq):
    n_in = 7 * len(B_GROUPS) + 1
    gate_ref = refs[n_in - 1]
    o_ref = refs[n_in]
    kw_ref, vw_ref, og_ref, mg_ref, lg_ref, bias_ref = refs[n_in + 1:]

    @pl.when((pl.program_id(0) == 0) & (pl.program_id(1) == 0) & (pl.program_id(2) == 0))
    def _():
        win = QBLK + 2 * HALO
        row_i = lax.broadcasted_iota(jnp.int32, (2 * QBLK, win), 0)
        col_i = lax.broadcasted_iota(jnp.int32, (2 * QBLK, win), 1)
        qpos = jnp.where(row_i >= QBLK, row_i - QBLK, row_i)
        band = jnp.where(jnp.abs(col_i - HALO - qpos) <= RADIUS, 0.0, MASKED)
        low = jnp.where(col_i < HALO, MASKED, 0.0)
        high = jnp.where(col_i >= QBLK + HALO, MASKED, 0.0)
        bias_ref[0] = band
        bias_ref[1] = band + low
        bias_ref[2] = band + high
        bias_ref[3] = band + low + high

    for gi, (_, dil) in enumerate(B_GROUPS):
        _attn_b_group(gi, dil, seq, *refs[7 * gi:7 * gi + 7], kw_ref, vw_ref, og_ref, mg_ref, lg_ref, bias_ref)
    m0, m1, m2 = mg_ref[0], mg_ref[1], mg_ref[2]
    mx = jnp.maximum(jnp.maximum(m0, m1), m2)
    e0, e1, e2 = jnp.exp2(m0 - mx), jnp.exp2(m1 - mx), jnp.exp2(m2 - mx)
    den = e0 * lg_ref[0] + e1 * lg_ref[1] + e2 * lg_ref[2]
    o = (e0 * og_ref[0] + e1 * og_ref[1] + e2 * og_ref[2]) / den
    o_ref[...] = (o * _silu(gate_ref[...].astype(F32))).astype(BF16)


def _attn_b(group_planes, gate_planes, gate_base, *, batch, seq):
    n_tiles = seq // TILE
    in_specs, args = [], []
    for (_, dil), planes in zip(B_GROUPS, group_planes):
        rows = TILE // dil
        nb64 = rows // HALO
        view = planes.reshape(planes.shape[0], batch, n_tiles, dil, nb64, HALO, LANES)

        def main(s):
            return pl.BlockSpec((None, None, None, dil, nb64, HALO, LANES),
                                lambda b, h, i, s=s: (s * SECTION_PLANES + h, b, i, 0, 0, 0, 0))

        def prev(s, nb64=nb64):
            return pl.BlockSpec((None, None, None, dil, None, HALO, LANES),
                                lambda b, h, i, s=s: (s * SECTION_PLANES + h, b, jnp.maximum(i - 1, 0), 0, nb64 - 1, 0, 0))

        def nxt(s):
            return pl.BlockSpec((None, None, None, dil, None, HALO, LANES),
                                lambda b, h, i, s=s: (s * SECTION_PLANES + h, b, jnp.minimum(i + 1, n_tiles - 1), 0, 0, 0, 0))

        in_specs += [main(0), main(1), prev(1), nxt(1), main(2), prev(2), nxt(2)]
        args += [view] * 7
    gate_view = gate_planes.reshape(gate_planes.shape[0], batch, seq, LANES)
    in_specs.append(pl.BlockSpec((None, None, TILE, LANES), lambda b, h, i: (gate_base + h, b, i, 0)))
    args.append(gate_view)
    max_dil = max(d for _, d in B_GROUPS)
    return pl.pallas_call(
        functools.partial(_attn_b_kernel, seq=seq),
        grid=(batch, N_PAIRS, n_tiles),
        in_specs=in_specs,
        out_specs=pl.BlockSpec((None, TILE, LANES), lambda b, h, i: (b, i, h)),
        out_shape=jax.ShapeDtypeStruct((batch, seq, D_MODEL), BF16),
        scratch_shapes=[
            pltpu.VMEM((max_dil, TILE + 2 * HALO, LANES), BF16),
            pltpu.VMEM((max_dil, TILE + 2 * HALO, LANES), BF16),
            pltpu.VMEM((len(B_GROUPS), TILE, LANES), F32),
            pltpu.VMEM((len(B_GROUPS), TILE, LANES), F32),
            pltpu.VMEM((len(B_GROUPS), TILE, LANES), F32),
            pltpu.VMEM((4, 2 * QBLK, QBLK + 2 * HALO), F32),
        ],
        compiler_params=_cparams(("arbitrary", "arbitrary", "arbitrary")),
    )(*args)


def _out_ln_kernel(y_ref, w_ref, x_ref, g_ref, b_ref, o_ref, *, tm, rc):
    for r0 in range(0, tm, rc):
        rows = slice(r0, r0 + rc)
        f = jnp.dot(y_ref[rows, :], w_ref[...], preferred_element_type=F32)
        z = ALPHA * x_ref[rows, :] + f
        mu = jnp.mean(z, axis=1, keepdims=True)
        zc = z - mu
        var = jnp.mean(zc * zc, axis=1, keepdims=True)
        o_ref[rows, :] = zc * lax.rsqrt(var + LN_EPS) * g_ref[...] + b_ref[...]


def _out_ln(y2d, w, x2d, g, b, *, tm=1024, rc=512):
    n = x2d.shape[0]
    return pl.pallas_call(
        functools.partial(_out_ln_kernel, tm=tm, rc=rc),
        grid=(n // tm,),
        in_specs=[
            pl.BlockSpec((tm, D_MODEL), lambda i: (i, 0)),
            pl.BlockSpec((D_MODEL, D_MODEL), lambda i: (0, 0)),
            pl.BlockSpec((tm, D_MODEL), lambda i: (i, 0)),
            pl.BlockSpec((1, D_MODEL), lambda i: (0, 0)),
            pl.BlockSpec((1, D_MODEL), lambda i: (0, 0)),
        ],
        out_specs=pl.BlockSpec((tm, D_MODEL), lambda i: (i, 0)),
        out_shape=jax.ShapeDtypeStruct((n, D_MODEL), F32),
        compiler_params=_cparams(("arbitrary",)),
    )(y2d, w, x2d, g.reshape(1, D_MODEL), b.reshape(1, D_MODEL))


Q_SCALE_LOG2 = math.log2(math.e) / math.sqrt(HEAD_DIM)


def _scale_q(w):
    return jnp.concatenate([w[:, :D_MODEL] * Q_SCALE_LOG2, w[:, D_MODEL:]], axis=1).astype(BF16)


def _layer_a(x, p, layer_idx):
    batch, seq, _ = x.shape
    x2d = x.reshape(batch * seq, D_MODEL)
    w_in = _scale_q(p["w_in"])
    planes = _project(x2d, w_in, seq=seq, dil=1, tile=1024)
    lam_init = 0.8 - 0.6 * math.exp(-0.3 * layer_idx)
    lamv = jnp.stack([p["lam_q1"], p["lam_k1"], p["lam_q2"], p["lam_k2"]]).astype(F32)
    y = _attn_a(planes, lamv, p["subln_g"].astype(F32).reshape(1, LANES), batch=batch, seq=seq, lam_init=lam_init)
    out = _out_ln(y.reshape(batch * seq, D_MODEL), p["w_out"].astype(BF16), x2d, p["ln_g"], p["ln_b"])
    return out.reshape(batch, seq, D_MODEL)


def _layer_b(x, p):
    batch, seq, _ = x.shape
    x2d = x.reshape(batch * seq, D_MODEL)
    w_in = p["w_in"]
    group_planes = []
    gate_planes = None
    for gi, (_, dil) in enumerate(B_GROUPS):
        w = w_in[:, 3 * gi * D_MODEL:3 * (gi + 1) * D_MODEL]
        if dil == 1:
            w = jnp.concatenate([w, w_in[:, 3 * len(B_GROUPS) * D_MODEL:]], axis=1)
        planes = _project(x2d, _scale_q(w), seq=seq, dil=dil, tile=TILE)
        if dil == 1:
            gate_planes = planes
        group_planes.append(planes)
    y = _attn_b(group_planes, gate_planes, 3 * SECTION_PLANES, batch=batch, seq=seq)
    out = _out_ln(y.reshape(batch * seq, D_MODEL), p["w_out"].astype(BF16), x2d, p["ln_g"], p["ln_b"])
    return out.reshape(batch, seq, D_MODEL)


def _trunk(x, layers):
    for i, p in enumerate(layers):
        x = _layer_a(x, p, i) if i % 2 == 0 else _layer_b(x, p)
    return x


def kernel(x_prompt, x_sample, w_in_0, lam_q1_0, lam_k1_0, lam_q2_0, lam_k2_0, subln_g_0, w_out_0, ln_g_0, ln_b_0, w_in_1, w_out_1, ln_g_1, ln_b_1, w_in_2, lam_q1_2, lam_k1_2, lam_q2_2, lam_k2_2, subln_g_2, w_out_2, ln_g_2, ln_b_2, w_in_3, w_out_3, ln_g_3, ln_b_3):
    layers = [
        dict(w_in=w_in_0, lam_q1=lam_q1_0, lam_k1=lam_k1_0, lam_q2=lam_q2_0, lam_k2=lam_k2_0,
             subln_g=subln_g_0, w_out=w_out_0, ln_g=ln_g_0, ln_b=ln_b_0),
        dict(w_in=w_in_1, w_out=w_out_1, ln_g=ln_g_1, ln_b=ln_b_1),
        dict(w_in=w_in_2, lam_q1=lam_q1_2, lam_k1=lam_k1_2, lam_q2=lam_q2_2, lam_k2=lam_k2_2,
             subln_g=subln_g_2, w_out=w_out_2, ln_g=ln_g_2, ln_b=ln_b_2),
        dict(w_in=w_in_3, w_out=w_out_3, ln_g=ln_g_3, ln_b=ln_b_3),
    ]
    return (_trunk(x_prompt, layers), _trunk(x_sample, layers))
```

```python
import functools
import math

import jax
import jax.numpy as jnp
from jax import lax
from jax.experimental import pallas as pl
from jax.experimental.pallas import tpu as pltpu

F32 = jnp.float32
BF16 = jnp.bfloat16

D_MODEL = 1024
DEPTH = 4
ROPE_THETA = 500000.0
LN_EPS = 1e-5
ALPHA = (2.0 * DEPTH) ** 0.25
HEAD_DIM = 64
LANES = 128
N_PAIRS = D_MODEL // LANES
SECTION_PLANES = D_MODEL // LANES
B_GROUPS = ((128, 1), (512, 4), (2048, 16))
RADIUS = 64
TILE = 2048
HALO = 64
QBLK = 128
MASKED = -1e30
GROUP = 4
VMEM_LIMIT = 56 * 1024 * 1024


def _cparams(sem, flags=None):
    return pltpu.CompilerParams(dimension_semantics=sem, vmem_limit_bytes=VMEM_LIMIT, flags=flags)


def _rope_tables(seq, dil, tm):
    half = HEAD_DIM // 8
    pos = jnp.arange(seq, dtype=jnp.int32).reshape(seq // tm, tm // dil, dil).transpose(0, 2, 1).reshape(seq)
    dim = jnp.arange(LANES, dtype=jnp.int32) % HEAD_DIM
    rotated = dim < 2 * half
    inv = jnp.where(rotated, ROPE_THETA ** (-(dim % half).astype(F32) / half), 0.0)
    ang = pos.astype(F32)[:, None] * inv[None, :]
    sin = jnp.sin(ang)
    c = jnp.cos(ang)
    sa = sin * jnp.where(dim < half, -1.0, 0.0)[None, :]
    sb = sin * jnp.where(rotated & (dim >= half), 1.0, 0.0)[None, :]
    return c, sa, sb


X_SLABS = D_MODEL // LANES


ROPE_COLS = 2 * D_MODEL


def _proj_kernel(*refs, dil, tm, sub_n, n_out):
    x_refs = refs[:X_SLABS]
    w_ref, c_ref, sa_ref, sb_ref, o_ref, xs_ref = refs[X_SLABS:]
    res_rows = tm // dil

    for c in range(X_SLABS):
        for r in range(dil):
            src = x_refs[c][pl.ds(r, res_rows, stride=dil), :] if dil > 1 else x_refs[c][...]
            xs_ref[r * res_rows:(r + 1) * res_rows, c * LANES:(c + 1) * LANES] = src.astype(BF16)

    for sd in range(n_out // sub_n):
        acc = jnp.dot(xs_ref[...], w_ref[:, sd * sub_n:(sd + 1) * sub_n], preferred_element_type=F32)
        for t in range(sub_n // LANES):
            a = acc[:, t * LANES:(t + 1) * LANES]
            if sd * sub_n < ROPE_COLS:
                a = a * c_ref[...] + pltpu.roll(a, LANES - 8, 1) * sa_ref[...] + pltpu.roll(a, 8, 1) * sb_ref[...]
            ab = a.astype(BF16)
            for r in range(dil):
                o_ref[sd * (sub_n // LANES) + t, r] = ab[r * res_rows:(r + 1) * res_rows]


def _project(x2d, w, *, seq, dil, tile, tm=1024, sub_n=512):
    n = x2d.shape[0]
    n_out = w.shape[1]
    per_tile = tile // tm
    n_seq_tiles = seq // tm
    tabs = _rope_tables(seq, dil, tm)
    kern = functools.partial(_proj_kernel, dil=dil, tm=tm, sub_n=sub_n, n_out=n_out)
    tab_spec = pl.BlockSpec((tm, LANES), lambda i: (i % n_seq_tiles, 0))
    x_specs = [pl.BlockSpec((tm, LANES), lambda i, c=c: (i, c)) for c in range(X_SLABS)]
    return pl.pallas_call(
        kern,
        grid=(n // tm,),
        in_specs=x_specs + [
            pl.BlockSpec((D_MODEL, n_out), lambda i: (0, 0)),
            tab_spec, tab_spec, tab_spec,
        ],
        out_specs=pl.BlockSpec((n_out // LANES, None, dil, tm // dil, LANES),
                               lambda i: (0, i // per_tile, 0, i % per_tile, 0)),
        out_shape=jax.ShapeDtypeStruct((n_out // LANES, n // tile, dil, tile // dil, LANES), BF16),
        scratch_shapes=[pltpu.VMEM((tm, D_MODEL), BF16)],
        compiler_params=_cparams(("arbitrary",)),
    )(*([x2d] * X_SLABS), w, *tabs)


N_SCORE_BUFS = 2


def _silu(g):
    return g * (1.0 / (1.0 + jnp.exp(-g)))


def _attn_a_kernel(lamv_ref, subg_ref, q_ref, k_ref, v_ref, g_ref, o_ref,
                   qt_ref, vt_ref, s0_ref, s1_ref, m_ref, l_ref, acc_ref, *, lam_init, bq, bk, nkv, nblk):
    @pl.when(pl.program_id(2) == 0)
    def _():
        for j in range(nkv):
            vt_ref[:, j * bk:(j + 1) * bk] = v_ref[j * bk:(j + 1) * bk, :].astype(F32).T.astype(BF16)

    for blk in range(nblk):
        qt = q_ref[blk * bq:(blk + 1) * bq, :].astype(F32).T
        dim = lax.broadcasted_iota(jnp.int32, qt.shape, 0)
        qt_ref[blk, :, 0:bq] = jnp.where(dim < HEAD_DIM, qt, 0.0).astype(BF16)
        qt_ref[blk, :, bq:2 * bq] = jnp.where(dim >= HEAD_DIM, qt, 0.0).astype(BF16)
    m_ref[...] = jnp.full(m_ref.shape, -jnp.inf, F32)
    l_ref[...] = jnp.zeros(l_ref.shape, F32)
    acc_ref[...] = jnp.zeros(acc_ref.shape, F32)

    def scores(blk, j, s_ref):
        kc = k_ref[j * bk:(j + 1) * bk, :]
        s_ref[...] = jnp.dot(kc, qt_ref[blk], preferred_element_type=F32)

    def update(blk, j, s_ref):
        m_prev = m_ref[blk]
        m_new = jnp.maximum(m_prev, jnp.max(s_ref[...], axis=0, keepdims=True))
        alpha = jnp.exp2(m_prev - m_new)
        p = jnp.exp2(s_ref[...] - m_new)
        l_ref[blk] = alpha * l_ref[blk] + jnp.sum(p, axis=0, keepdims=True)
        pv = jnp.dot(vt_ref[:, j * bk:(j + 1) * bk], p.astype(BF16), preferred_element_type=F32)
        acc_ref[blk] = alpha * acc_ref[blk] + pv
        m_ref[blk] = m_new

    def finalize(blk):
        lv = lamv_ref[...]
        lam = (jnp.exp(jnp.sum(lv[0:1] * lv[1:2], axis=1, keepdims=True))
               - jnp.exp(jnp.sum(lv[2:3] * lv[3:4], axis=1, keepdims=True)) + lam_init)
        ot = acc_ref[blk] / l_ref[blk]
        odt = ot[:, 0:bq] - lam * ot[:, bq:2 * bq]
        odt = odt * lax.rsqrt(jnp.mean(odt * odt, axis=0, keepdims=True) + LN_EPS)
        od = odt.T * subg_ref[...] * (1.0 - lam_init)
        rows = slice(blk * bq, (blk + 1) * bq)
        o_ref[rows, :] = (od * _silu(g_ref[rows, :].astype(F32))).astype(BF16)

    work = [(blk, j) for blk in range(nblk) for j in range(nkv)]
    bufs = (s0_ref, s1_ref)
    scores(*work[0], bufs[0])
    for i, (blk, j) in enumerate(work):
        if i + 1 < len(work):
            scores(*work[i + 1], bufs[(i + 1) % N_SCORE_BUFS])
        update(blk, j, bufs[i % N_SCORE_BUFS])
        if j == nkv - 1:
            finalize(blk)


def _attn_a(planes, lamv, subg, *, batch, seq, lam_init, bq=512, bk=1024, nblk=2):
    pl4 = planes.reshape(4 * SECTION_PLANES, batch, seq, LANES)
    nkv = seq // bk
    rows = nblk * bq
    assert seq % bk == 0 and seq % rows == 0
    kern = functools.partial(_attn_a_kernel, lam_init=lam_init, bq=bq, bk=bk, nkv=nkv, nblk=nblk)

    def q_rows(s):
        return pl.BlockSpec((None, None, rows, LANES), lambda b, h, qi: (s * SECTION_PLANES + h, b, qi, 0))

    def all_rows(s):
        return pl.BlockSpec((None, None, seq, LANES), lambda b, h, qi: (s * SECTION_PLANES + h, b, 0, 0))

    return pl.pallas_call(
        kern,
        grid=(batch, N_PAIRS, seq // rows),
        in_specs=[
            pl.BlockSpec((4, HEAD_DIM), lambda b, h, qi: (0, 0)),
            pl.BlockSpec((1, LANES), lambda b, h, qi: (0, 0)),
            q_rows(0), all_rows(1), all_rows(2), q_rows(3),
        ],
        out_specs=pl.BlockSpec((None, rows, LANES), lambda b, h, qi: (b, qi, h)),
        out_shape=jax.ShapeDtypeStruct((batch, seq, D_MODEL), BF16),
        scratch_shapes=[
            pltpu.VMEM((nblk, LANES, 2 * bq), BF16),
            pltpu.VMEM((LANES, seq), BF16),
        ] + [pltpu.VMEM((bk, 2 * bq), F32)] * N_SCORE_BUFS + [
            pltpu.VMEM((nblk, 1, 2 * bq), F32),
            pltpu.VMEM((nblk, 1, 2 * bq), F32),
            pltpu.VMEM((nblk, LANES, 2 * bq), F32),
        ],
        compiler_params=_cparams(("arbitrary", "arbitrary", "arbitrary")),
    )(lamv, subg, pl4, pl4, pl4, pl4)


def _attn_b_group(gi, dil, seq, q_ref, k_ref, kp_ref, kn_ref, v_ref, vp_ref, vn_ref,
                  kw_ref, vw_ref, og_ref, mg_ref, lg_ref, bias_ref):
    rows = TILE // dil
    nb64 = rows // HALO
    tile_i = pl.program_id(2)
    length = seq // dil

    kw_ref[0:dil, 0:HALO, :] = kp_ref[...]
    vw_ref[0:dil, 0:HALO, :] = vp_ref[...]
    for n in range(nb64):
        kw_ref[0:dil, HALO + n * HALO:2 * HALO + n * HALO, :] = k_ref[:, n]
        vw_ref[0:dil, HALO + n * HALO:2 * HALO + n * HALO, :] = v_ref[:, n]
    kw_ref[0:dil, HALO + rows:2 * HALO + rows, :] = kn_ref[...]
    vw_ref[0:dil, HALO + rows:2 * HALO + rows, :] = vn_ref[...]

    win = QBLK + 2 * HALO
    lane_q = lax.broadcasted_iota(jnp.int32, (QBLK, LANES), 1)
    lo_q = lane_q < HEAD_DIM
    blocks = rows // QBLK
    unroll = max(1, 8 // blocks)

    def block_scores(r, blk):
        q2 = q_ref[r, 2 * blk:2 * blk + 2].reshape(QBLK, LANES)
        zero = jnp.zeros_like(q2)
        qs = jnp.concatenate([jnp.where(lo_q, q2, zero), jnp.where(lo_q, zero, q2)], axis=0)
        kwin = kw_ref[r, blk * QBLK:blk * QBLK + win, :]
        s = lax.dot_general(qs, kwin, (((1,), (1,)), ((), ())), preferred_element_type=F32)
        first = tile_i * rows + blk * QBLK
        edge = (first == 0).astype(jnp.int32) + 2 * (first + QBLK == length).astype(jnp.int32)
        return s + bias_ref[edge]

    def block_softmax(s):
        m = jnp.max(s, axis=1, keepdims=True)
        p = jnp.exp2(s - m)
        return p.astype(BF16), m, jnp.sum(p, axis=1, keepdims=True)

    def per_head(a):
        return jnp.where(lo_q, jnp.broadcast_to(a[0:QBLK], (QBLK, LANES)),
                         jnp.broadcast_to(a[QBLK:2 * QBLK], (QBLK, LANES)))

    def block_output(r, blk, pb, m, l):
        vwin = vw_ref[r, blk * QBLK:blk * QBLK + win, :]
        o0 = jnp.dot(pb[0:QBLK], vwin, preferred_element_type=F32)
        o1 = jnp.dot(pb[QBLK:2 * QBLK], vwin, preferred_element_type=F32)
        tok = pl.ds(r + blk * QBLK * dil, QBLK, stride=dil) if dil > 1 else pl.ds(blk * QBLK, QBLK)
        og_ref[gi, tok, :] = jnp.where(lo_q, o0, o1)
        mg_ref[gi, tok, :] = per_head(m)
        lg_ref[gi, tok, :] = per_head(l)

    def body(it, carry):
        items = [(it * unroll + u, blk) for u in range(unroll) for blk in range(blocks)]
        for g0 in range(0, len(items), GROUP):
            group = items[g0:g0 + GROUP]
            scores = [block_scores(r, blk) for r, blk in group]
            soft = [block_softmax(s) for s in scores]
            for (r, blk), (pb, m, l) in zip(group, soft):
                block_output(r, blk, pb, m, l)
        return carry

    if dil == 1:
        body(0, 0)
    else:
        lax.fori_loop(0, dil // unroll, body, 0)


def _attn_b_kernel(*refs, seq):
    n_in = 7 * len(B_GROUPS) + 1
    gate_ref = refs[n_in - 1]
    o_ref = refs[n_in]
    kw_ref, vw_ref, og_ref, mg_ref, lg_ref, bias_ref = refs[n_in + 1:]

    @pl.when((pl.program_id(0) == 0) & (pl.program_id(1) == 0) & (pl.program_id(2) == 0))
    def _():
        win = QBLK + 2 * HALO
        row_i = lax.broadcasted_iota(jnp.int32, (2 * QBLK, win), 0)
        col_i = lax.broadcasted_iota(jnp.int32, (2 * QBLK, win), 1)
        qpos = jnp.where(row_i >= QBLK, row_i - QBLK, row_i)
        band = jnp.where(jnp.abs(col_i - HALO - qpos) <= RADIUS, 0.0, MASKED)
        low = jnp.where(col_i < HALO, MASKED, 0.0)
        high = jnp.where(col_i >= QBLK + HALO, MASKED, 0.0)
        bias_ref[0] = band
        bias_ref[1] = band + low
        bias_ref[2] = band + high
        bias_ref[3] = band + low + high

    for gi, (_, dil) in enumerate(B_GROUPS):
        _attn_b_group(gi, dil, seq, *refs[7 * gi:7 * gi + 7], kw_ref, vw_ref, og_ref, mg_ref, lg_ref, bias_ref)
    m0, m1, m2 = mg_ref[0], mg_ref[1], mg_ref[2]
    mx = jnp.maximum(jnp.maximum(m0, m1), m2)
    e0, e1, e2 = jnp.exp2(m0 - mx), jnp.exp2(m1 - mx), jnp.exp2(m2 - mx)
    den = e0 * lg_ref[0] + e1 * lg_ref[1] + e2 * lg_ref[2]
    o = (e0 * og_ref[0] + e1 * og_ref[1] + e2 * og_ref[2]) / den
    o_ref[...] = (o * _silu(gate_ref[...].astype(F32))).astype(BF16)


def _attn_b(group_planes, gate_planes, gate_base, *, batch, seq):
    n_tiles = seq // TILE
    in_specs, args = [], []
    for (_, dil), planes in zip(B_GROUPS, group_planes):
        rows = TILE // dil
        nb64 = rows // HALO
        view = planes.reshape(planes.shape[0], batch, n_tiles, dil, nb64, HALO, LANES)

        def main(s):
            return pl.BlockSpec((None, None, None, dil, nb64, HALO, LANES),
                                lambda b, h, i, s=s: (s * SECTION_PLANES + h, b, i, 0, 0, 0, 0))

        def prev(s, nb64=nb64):
            return pl.BlockSpec((None, None, None, dil, None, HALO, LANES),
                                lambda b, h, i, s=s: (s * SECTION_PLANES + h, b, jnp.maximum(i - 1, 0), 0, nb64 - 1, 0, 0))

        def nxt(s):
            return pl.BlockSpec((None, None, None, dil, None, HALO, LANES),
                                lambda b, h, i, s=s: (s * SECTION_PLANES + h, b, jnp.minimum(i + 1, n_tiles - 1), 0, 0, 0, 0))

        in_specs += [main(0), main(1), prev(1), nxt(1), main(2), prev(2), nxt(2)]
        args += [view] * 7
    gate_view = gate_planes.reshape(gate_planes.shape[0], batch, seq, LANES)
    in_specs.append(pl.BlockSpec((None, None, TILE, LANES), lambda b, h, i: (gate_base + h, b, i, 0)))
    args.append(gate_view)
    max_dil = max(d for _, d in B_GROUPS)
    return pl.pallas_call(
        functools.partial(_attn_b_kernel, seq=seq),
        grid=(batch, N_PAIRS, n_tiles),
        in_specs=in_specs,
        out_specs=pl.BlockSpec((None, TILE, LANES), lambda b, h, i: (b, i, h)),
        out_shape=jax.ShapeDtypeStruct((batch, seq, D_MODEL), BF16),
        scratch_shapes=[
            pltpu.VMEM((max_dil, TILE + 2 * HALO, LANES), BF16),
            pltpu.VMEM((max_dil, TILE + 2 * HALO, LANES), BF16),
            pltpu.VMEM((len(B_GROUPS), TILE, LANES), F32),
            pltpu.VMEM((len(B_GROUPS), TILE, LANES), F32),
            pltpu.VMEM((len(B_GROUPS), TILE, LANES), F32),
            pltpu.VMEM((4, 2 * QBLK, QBLK + 2 * HALO), F32),
        ],
        compiler_params=_cparams(("arbitrary", "arbitrary", "arbitrary")),
    )(*args)


def _out_ln_kernel(y_ref, w_ref, x_ref, g_ref, b_ref, o_ref, *, tm, rc):
    for r0 in range(0, tm, rc):
        rows = slice(r0, r0 + rc)
        f = jnp.dot(y_ref[rows, :], w_ref[...], preferred_element_type=F32)
        z = ALPHA * x_ref[rows, :] + f
        mu = jnp.mean(z, axis=1, keepdims=True)
        zc = z - mu
        var = jnp.mean(zc * zc, axis=1, keepdims=True)
        o_ref[rows, :] = zc * lax.rsqrt(var + LN_EPS) * g_ref[...] + b_ref[...]


def _out_ln(y2d, w, x2d, g, b, *, tm=1024, rc=512):
    n = x2d.shape[0]
    return pl.pallas_call(
        functools.partial(_out_ln_kernel, tm=tm, rc=rc),
        grid=(n // tm,),
        in_specs=[
            pl.BlockSpec((tm, D_MODEL), lambda i: (i, 0)),
            pl.BlockSpec((D_MODEL, D_MODEL), lambda i: (0, 0)),
            pl.BlockSpec((tm, D_MODEL), lambda i: (i, 0)),
            pl.BlockSpec((1, D_MODEL), lambda i: (0, 0)),
            pl.BlockSpec((1, D_MODEL), lambda i: (0, 0)),
        ],
        out_specs=pl.BlockSpec((tm, D_MODEL), lambda i: (i, 0)),
        out_shape=jax.ShapeDtypeStruct((n, D_MODEL), F32),
        compiler_params=_cparams(("arbitrary",)),
    )(y2d, w, x2d, g.reshape(1, D_MODEL), b.reshape(1, D_MODEL))


Q_SCALE_LOG2 = math.log2(math.e) / math.sqrt(HEAD_DIM)


def _scale_q(w):
    return jnp.concatenate([w[:, :D_MODEL] * Q_SCALE_LOG2, w[:, D_MODEL:]], axis=1).astype(BF16)


def _layer_a(x, p, layer_idx):
    batch, seq, _ = x.shape
    x2d = x.reshape(batch * seq, D_MODEL)
    w_in = _scale_q(p["w_in"])
    planes = _project(x2d, w_in, seq=seq, dil=1, tile=1024)
    lam_init = 0.8 - 0.6 * math.exp(-0.3 * layer_idx)
    lamv = jnp.stack([p["lam_q1"], p["lam_k1"], p["lam_q2"], p["lam_k2"]]).astype(F32)
    y = _attn_a(planes, lamv, p["subln_g"].astype(F32).reshape(1, LANES), batch=batch, seq=seq, lam_init=lam_init)
    out = _out_ln(y.reshape(batch * seq, D_MODEL), p["w_out"].astype(BF16), x2d, p["ln_g"], p["ln_b"])
    return out.reshape(batch, seq, D_MODEL)


def _layer_b(x, p):
    batch, seq, _ = x.shape
    x2d = x.reshape(batch * seq, D_MODEL)
    w_in = p["w_in"]
    group_planes = []
    gate_planes = None
    for gi, (_, dil) in enumerate(B_GROUPS):
        w = w_in[:, 3 * gi * D_MODEL:3 * (gi + 1) * D_MODEL]
        if dil == 1:
            w = jnp.concatenate([w, w_in[:, 3 * len(B_GROUPS) * D_MODEL:]], axis=1)
        planes = _project(x2d, _scale_q(w), seq=seq, dil=dil, tile=TILE)
        if dil == 1:
            gate_planes = planes
        group_planes.append(planes)
    y = _attn_b(group_planes, gate_planes, 3 * SECTION_PLANES, batch=batch, seq=seq)
    out = _out_ln(y.reshape(batch * seq, D_MODEL), p["w_out"].astype(BF16), x2d, p["ln_g"], p["ln_b"])
    return out.reshape(batch, seq, D_MODEL)


def _trunk(x, layers):
    for i, p in enumerate(layers):
        x = _layer_a(x, p, i) if i % 2 == 0 else _layer_b(x, p)
    return x


def kernel(x_prompt, x_sample, w_in_0, lam_q1_0, lam_k1_0, lam_q2_0, lam_k2_0, subln_g_0, w_out_0, ln_g_0, ln_b_0, w_in_1, w_out_1, ln_g_1, ln_b_1, w_in_2, lam_q1_2, lam_k1_2, lam_q2_2, lam_k2_2, subln_g_2, w_out_2, ln_g_2, ln_b_2, w_in_3, w_out_3, ln_g_3, ln_b_3):
    layers = [
        dict(w_in=w_in_0, lam_q1=lam_q1_0, lam_k1=lam_k1_0, lam_q2=lam_q2_0, lam_k2=lam_k2_0,
             subln_g=subln_g_0, w_out=w_out_0, ln_g=ln_g_0, ln_b=ln_b_0),
        dict(w_in=w_in_1, w_out=w_out_1, ln_g=ln_g_1, ln_b=ln_b_1),
        dict(w_in=w_in_2, lam_q1=lam_q1_2, lam_k1=lam_k1_2, lam_q2=lam_q2_2, lam_k2=lam_k2_2,
             subln_g=subln_g_2, w_out=w_out_2, ln_g=ln_g_2, ln_b=ln_b_2),
        dict(w_in=w_in_3, w_out=w_out_3, ln_g=ln_g_3, ln_b=ln_b_3),
    ]
    return (_trunk(x_prompt, layers), _trunk(x_sample, layers))
```

```python
import functools
import math

import jax
import jax.numpy as jnp
from jax import lax
from jax.experimental import pallas as pl
from jax.experimental.pallas import tpu as pltpu

F32 = jnp.float32
BF16 = jnp.bfloat16

D_MODEL = 1024
DEPTH = 4
ROPE_THETA = 500000.0
LN_EPS = 1e-5
ALPHA = (2.0 * DEPTH) ** 0.25
HEAD_DIM = 64
LANES = 128
N_PAIRS = D_MODEL // LANES
SECTION_PLANES = D_MODEL // LANES
B_GROUPS = ((128, 1), (512, 4), (2048, 16))
RADIUS = 64
TILE = 2048
HALO = 64
QBLK = 128
MASKED = -1e30
GROUP = 4
VMEM_LIMIT = 56 * 1024 * 1024


def _cparams(sem, flags=None):
    return pltpu.CompilerParams(dimension_semantics=sem, vmem_limit_bytes=VMEM_LIMIT, flags=flags)


def _rope_tables(seq, dil, tm):
    half = HEAD_DIM // 8
    pos = jnp.arange(seq, dtype=jnp.int32).reshape(seq // tm, tm // dil, dil).transpose(0, 2, 1).reshape(seq)
    dim = jnp.arange(LANES, dtype=jnp.int32) % HEAD_DIM
    rotated = dim < 2 * half
    inv = jnp.where(rotated, ROPE_THETA ** (-(dim % half).astype(F32) / half), 0.0)
    ang = pos.astype(F32)[:, None] * inv[None, :]
    sin = jnp.sin(ang)
    c = jnp.cos(ang)
    sa = sin * jnp.where(dim < half, -1.0, 0.0)[None, :]
    sb = sin * jnp.where(rotated & (dim >= half), 1.0, 0.0)[None, :]
    return c, sa, sb


X_SLABS = D_MODEL // LANES


ROPE_COLS = 2 * D_MODEL


def _proj_kernel(*refs, dil, tm, sub_n, n_out):
    x_refs = refs[:X_SLABS]
    w_ref, c_ref, sa_ref, sb_ref, o_ref, xs_ref = refs[X_SLABS:]
    res_rows = tm // dil

    for c in range(X_SLABS):
        for r in range(dil):
            src = x_refs[c][pl.ds(r, res_rows, stride=dil), :] if dil > 1 else x_refs[c][...]
            xs_ref[r * res_rows:(r + 1) * res_rows, c * LANES:(c + 1) * LANES] = src.astype(BF16)

    for sd in range(n_out // sub_n):
        acc = jnp.dot(xs_ref[...], w_ref[:, sd * sub_n:(sd + 1) * sub_n], preferred_element_type=F32)
        for t in range(sub_n // LANES):
            a = acc[:, t * LANES:(t + 1) * LANES]
            if sd * sub_n < ROPE_COLS:
                a = a * c_ref[...] + pltpu.roll(a, LANES - 8, 1) * sa_ref[...] + pltpu.roll(a, 8, 1) * sb_ref[...]
            ab = a.astype(BF16)
            for r in range(dil):
                o_ref[sd * (sub_n // LANES) + t, r] = ab[r * res_rows:(r + 1) * res_rows]


def _project(x2d, w, *, seq, dil, tile, tm=1024, sub_n=512):
    n = x2d.shape[0]
    n_out = w.shape[1]
    per_tile = tile // tm
    n_seq_tiles = seq // tm
    tabs = _rope_tables(seq, dil, tm)
    kern = functools.partial(_proj_kernel, dil=dil, tm=tm, sub_n=sub_n, n_out=n_out)
    tab_spec = pl.BlockSpec((tm, LANES), lambda i: (i % n_seq_tiles, 0))
    x_specs = [pl.BlockSpec((tm, LANES), lambda i, c=c: (i, c)) for c in range(X_SLABS)]
    return pl.pallas_call(
        kern,
        grid=(n // tm,),
        in_specs=x_specs + [
            pl.BlockSpec((D_MODEL, n_out), lambda i: (0, 0)),
            tab_spec, tab_spec, tab_spec,
        ],
        out_specs=pl.BlockSpec((n_out // LANES, None, dil, tm // dil, LANES),
                               lambda i: (0, i // per_tile, 0, i % per_tile, 0)),
        out_shape=jax.ShapeDtypeStruct((n_out // LANES, n // tile, dil, tile // dil, LANES), BF16),
        scratch_shapes=[pltpu.VMEM((tm, D_MODEL), BF16)],
        compiler_params=_cparams(("arbitrary",)),
    )(*([x2d] * X_SLABS), w, *tabs)


N_SCORE_BUFS = 2


def _silu(g):
    return g * (1.0 / (1.0 + jnp.exp(-g)))


def _attn_a_kernel(lamv_ref, subg_ref, q_ref, k_ref, v_ref, g_ref, o_ref,
                   qt_ref, vt_ref, s0_ref, s1_ref, m_ref, l_ref, acc_ref, *, lam_init, bq, bk, nkv, nblk):
    @pl.when(pl.program_id(2) == 0)
    def _():
        for j in range(nkv):
            vt_ref[:, j * bk:(j + 1) * bk] = v_ref[j * bk:(j + 1) * bk, :].astype(F32).T.astype(BF16)

    for blk in range(nblk):
        qt = q_ref[blk * bq:(blk + 1) * bq, :].astype(F32).T
        dim = lax.broadcasted_iota(jnp.int32, qt.shape, 0)
        qt_ref[blk, :, 0:bq] = jnp.where(dim < HEAD_DIM, qt, 0.0).astype(BF16)
        qt_ref[blk, :, bq:2 * bq] = jnp.where(dim >= HEAD_DIM, qt, 0.0).astype(BF16)
    m_ref[...] = jnp.full(m_ref.shape, -jnp.inf, F32)
    l_ref[...] = jnp.zeros(l_ref.shape, F32)
    acc_ref[...] = jnp.zeros(acc_ref.shape, F32)

    def scores(blk, j, s_ref):
        kc = k_ref[j * bk:(j + 1) * bk, :]
        s_ref[...] = jnp.dot(kc, qt_ref[blk], preferred_element_type=F32)

    def update(blk, j, s_ref):
        m_prev = m_ref[blk]
        m_new = jnp.maximum(m_prev, jnp.max(s_ref[...], axis=0, keepdims=True))
        alpha = jnp.exp2(m_prev - m_new)
        p = jnp.exp2(s_ref[...] - m_new)
        l_ref[blk] = alpha * l_ref[blk] + jnp.sum(p, axis=0, keepdims=True)
        pv = jnp.dot(vt_ref[:, j * bk:(j + 1) * bk], p.astype(BF16), preferred_element_type=F32)
        acc_ref[blk] = alpha * acc_ref[blk] + pv
        m_ref[blk] = m_new

    def finalize(blk):
        lv = lamv_ref[...]
        lam = (jnp.exp(jnp.sum(lv[0:1] * lv[1:2], axis=1, keepdims=True))
               - jnp.exp(jnp.sum(lv[2:3] * lv[3:4], axis=1, keepdims=True)) + lam_init)
        ot = acc_ref[blk] / l_ref[blk]
        odt = ot[:, 0:bq] - lam * ot[:, bq:2 * bq]
        odt = odt * lax.rsqrt(jnp.mean(odt * odt, axis=0, keepdims=True) + LN_EPS)
        od = odt.T * subg_ref[...] * (1.0 - lam_init)
        rows = slice(blk * bq, (blk + 1) * bq)
        o_ref[rows, :] = (od * _silu(g_ref[rows, :].astype(F32))).astype(BF16)

    work = [(blk, j) for blk in range(nblk) for j in range(nkv)]
    bufs = (s0_ref, s1_ref)
    scores(*work[0], bufs[0])
    for i, (blk, j) in enumerate(work):
        if i + 1 < len(work):
            scores(*work[i + 1], bufs[(i + 1) % N_SCORE_BUFS])
        update(blk, j, bufs[i % N_SCORE_BUFS])
        if j == nkv - 1:
            finalize(blk)


def _attn_a(planes, lamv, subg, *, batch, seq, lam_init, bq=512, bk=1024, nblk=2):
    pl4 = planes.reshape(4 * SECTION_PLANES, batch, seq, LANES)
    nkv = seq // bk
    rows = nblk * bq
    assert seq % bk == 0 and seq % rows == 0
    kern = functools.partial(_attn_a_kernel, lam_init=lam_init, bq=bq, bk=bk, nkv=nkv, nblk=nblk)

    def q_rows(s):
        return pl.BlockSpec((None, None, rows, LANES), lambda b, h, qi: (s * SECTION_PLANES + h, b, qi, 0))

    def all_rows(s):
        return pl.BlockSpec((None, None, seq, LANES), lambda b, h, qi: (s * SECTION_PLANES + h, b, 0, 0))

    return pl.pallas_call(
        kern,
        grid=(batch, N_PAIRS, seq // rows),
        in_specs=[
            pl.BlockSpec((4, HEAD_DIM), lambda b, h, qi: (0, 0)),
            pl.BlockSpec((1, LANES), lambda b, h, qi: (0, 0)),
            q_rows(0), all_rows(1), all_rows(2), q_rows(3),
        ],
        out_specs=pl.BlockSpec((None, rows, LANES), lambda b, h, qi: (b, qi, h)),
        out_shape=jax.ShapeDtypeStruct((batch, seq, D_MODEL), BF16),
        scratch_shapes=[
            pltpu.VMEM((nblk, LANES, 2 * bq), BF16),
            pltpu.VMEM((LANES, seq), BF16),
        ] + [pltpu.VMEM((bk, 2 * bq), F32)] * N_SCORE_BUFS + [
            pltpu.VMEM((nblk, 1, 2 * bq), F32),
            pltpu.VMEM((nblk, 1, 2 * bq), F32),
            pltpu.VMEM((nblk, LANES, 2 * bq), F32),
        ],
        compiler_params=_cparams(("arbitrary", "arbitrary", "arbitrary")),
    )(lamv, subg, pl4, pl4, pl4, pl4)


def _attn_b_group(gi, dil, seq, q_ref, k_ref, kp_ref, kn_ref, v_ref, vp_ref, vn_ref,
                  kw_ref, vw_ref, og_ref, mg_ref, lg_ref, bias_ref):
    rows = TILE // dil
    nb64 = rows // HALO
    tile_i = pl.program_id(2)
    length = seq // dil

    kw_ref[0:dil, 0:HALO, :] = kp_ref[...]
    vw_ref[0:dil, 0:HALO, :] = vp_ref[...]
    for n in range(nb64):
        kw_ref[0:dil, HALO + n * HALO:2 * HALO + n * HALO, :] = k_ref[:, n]
        vw_ref[0:dil, HALO + n * HALO:2 * HALO + n * HALO, :] = v_ref[:, n]
    kw_ref[0:dil, HALO + rows:2 * HALO + rows, :] = kn_ref[...]
    vw_ref[0:dil, HALO + rows:2 * HALO + rows, :] = vn_ref[...]

    win = QBLK + 2 * HALO
    lane_q = lax.broadcasted_iota(jnp.int32, (QBLK, LANES), 1)
    lo_q = lane_q < HEAD_DIM
    blocks = rows // QBLK
    unroll = max(1, 8 // blocks)

    def block_scores(r, blk):
        q2 = q_ref[r, 2 * blk:2 * blk + 2].reshape(QBLK, LANES)
        zero = jnp.zeros_like(q2)
        qs = jnp.concatenate([jnp.where(lo_q, q2, zero), jnp.where(lo_q, zero, q2)], axis=0)
        kwin = kw_ref[r, blk * QBLK:blk * QBLK + win, :]
        s = lax.dot_general(qs, kwin, (((1,), (1,)), ((), ())), preferred_element_type=F32)
        first = tile_i * rows + blk * QBLK
        edge = (first == 0).astype(jnp.int32) + 2 * (first + QBLK == length).astype(jnp.int32)
        return s + bias_ref[edge]

    def block_softmax(s):
        m = jnp.max(s, axis=1, keepdims=True)
        p = jnp.exp2(s - m)
        return p.astype(BF16), m, jnp.sum(p, axis=1, keepdims=True)

    def per_head(a):
        return jnp.where(lo_q, jnp.broadcast_to(a[0:QBLK], (QBLK, LANES)),
                         jnp.broadcast_to(a[QBLK:2 * QBLK], (QBLK, LANES)))

    def block_output(r, blk, pb, m, l):
        vwin = vw_ref[r, blk * QBLK:blk * QBLK + win, :]
        pv = jnp.dot(pb, vwin, preferred_element_type=F32)
        tok = pl.ds(r + blk * QBLK * dil, QBLK, stride=dil) if dil > 1 else pl.ds(blk * QBLK, QBLK)
        og_ref[gi, tok, :] = jnp.where(lo_q, pv[0:QBLK], pv[QBLK:2 * QBLK])
        mg_ref[gi, tok, :] = per_head(m)
        lg_ref[gi, tok, :] = per_head(l)

    def body(it, carry):
        items = [(it * unroll + u, blk) for u in range(unroll) for blk in range(blocks)]
        for g0 in range(0, len(items), GROUP):
            group = items[g0:g0 + GROUP]
            scores = [block_scores(r, blk) for r, blk in group]
            soft = [block_softmax(s) for s in scores]
            for (r, blk), (pb, m, l) in zip(group, soft):
                block_output(r, blk, pb, m, l)
        return carry

    if dil == 1:
        body(0, 0)
    else:
        lax.fori_loop(0, dil // unroll, body, 0)


def _attn_b_kernel(*refs, seq):
    n_in = 7 * len(B_GROUPS) + 1
    gate_ref = refs[n_in - 1]
    o_ref = refs[n_in]
    kw_ref, vw_ref, og_ref, mg_ref, lg_ref, bias_ref = refs[n_in + 1:]

    @pl.when((pl.program_id(0) == 0) & (pl.program_id(1) == 0) & (pl.program_id(2) == 0))
    def _():
        win = QBLK + 2 * HALO
        row_i = lax.broadcasted_iota(jnp.int32, (2 * QBLK, win), 0)
        col_i = lax.broadcasted_iota(jnp.int32, (2 * QBLK, win), 1)
        qpos = jnp.where(row_i >= QBLK, row_i - QBLK, row_i)
        band = jnp.where(jnp.abs(col_i - HALO - qpos) <= RADIUS, 0.0, MASKED)
        low = jnp.where(col_i < HALO, MASKED, 0.0)
        high = jnp.where(col_i >= QBLK + HALO, MASKED, 0.0)
        bias_ref[0] = band
        bias_ref[1] = band + low
        bias_ref[2] = band + high
        bias_ref[3] = band + low + high

    for gi, (_, dil) in enumerate(B_GROUPS):
        _attn_b_group(gi, dil, seq, *refs[7 * gi:7 * gi + 7], kw_ref, vw_ref, og_ref, mg_ref, lg_ref, bias_ref)
    m0, m1, m2 = mg_ref[0], mg_ref[1], mg_ref[2]
    mx = jnp.maximum(jnp.maximum(m0, m1), m2)
    e0, e1, e2 = jnp.exp2(m0 - mx), jnp.exp2(m1 - mx), jnp.exp2(m2 - mx)
    den = e0 * lg_ref[0] + e1 * lg_ref[1] + e2 * lg_ref[2]
    o = (e0 * og_ref[0] + e1 * og_ref[1] + e2 * og_ref[2]) / den
    o_ref[...] = (o * _silu(gate_ref[...].astype(F32))).astype(BF16)


def _attn_b(group_planes, gate_planes, gate_base, *, batch, seq):
    n_tiles = seq // TILE
    in_specs, args = [], []
    for (_, dil), planes in zip(B_GROUPS, group_planes):
        rows = TILE // dil
        nb64 = rows // HALO
        view = planes.reshape(planes.shape[0], batch, n_tiles, dil, nb64, HALO, LANES)

        def main(s):
            return pl.BlockSpec((None, None, None, dil, nb64, HALO, LANES),
                                lambda b, h, i, s=s: (s * SECTION_PLANES + h, b, i, 0, 0, 0, 0))

        def prev(s, nb64=nb64):
            return pl.BlockSpec((None, None, None, dil, None, HALO, LANES),
                                lambda b, h, i, s=s: (s * SECTION_PLANES + h, b, jnp.maximum(i - 1, 0), 0, nb64 - 1, 0, 0))

        def nxt(s):
            return pl.BlockSpec((None, None, None, dil, None, HALO, LANES),
                                lambda b, h, i, s=s: (s * SECTION_PLANES + h, b, jnp.minimum(i + 1, n_tiles - 1), 0, 0, 0, 0))

        in_specs += [main(0), main(1), prev(1), nxt(1), main(2), prev(2), nxt(2)]
        args += [view] * 7
    gate_view = gate_planes.reshape(gate_planes.shape[0], batch, seq, LANES)
    in_specs.append(pl.BlockSpec((None, None, TILE, LANES), lambda b, h, i: (gate_base + h, b, i, 0)))
    args.append(gate_view)
    max_dil = max(d for _, d in B_GROUPS)
    return pl.pallas_call(
        functools.partial(_attn_b_kernel, seq=seq),
        grid=(batch, N_PAIRS, n_tiles),
        in_specs=in_specs,
        out_specs=pl.BlockSpec((None, TILE, LANES), lambda b, h, i: (b, i, h)),
        out_shape=jax.ShapeDtypeStruct((batch, seq, D_MODEL), BF16),
        scratch_shapes=[
            pltpu.VMEM((max_dil, TILE + 2 * HALO, LANES), BF16),
            pltpu.VMEM((max_dil, TILE + 2 * HALO, LANES), BF16),
            pltpu.VMEM((len(B_GROUPS), TILE, LANES), F32),
            pltpu.VMEM((len(B_GROUPS), TILE, LANES), F32),
            pltpu.VMEM((len(B_GROUPS), TILE, LANES), F32),
            pltpu.VMEM((4, 2 * QBLK, QBLK + 2 * HALO), F32),
        ],
        compiler_params=_cparams(("arbitrary", "arbitrary", "arbitrary")),
    )(*args)


def _out_ln_kernel(y_ref, w_ref, x_ref, g_ref, b_ref, o_ref, *, tm, rc):
    for r0 in range(0, tm, rc):
        rows = slice(r0, r0 + rc)
        f = jnp.dot(y_ref[rows, :], w_ref[...], preferred_element_type=F32)
        z = ALPHA * x_ref[rows, :] + f
        mu = jnp.mean(z, axis=1, keepdims=True)
        zc = z - mu
        var = jnp.mean(zc * zc, axis=1, keepdims=True)
        o_ref[rows, :] = zc * lax.rsqrt(var + LN_EPS) * g_ref[...] + b_ref[...]


def _out_ln(y2d, w, x2d, g, b, *, tm=1024, rc=512):
    n = x2d.shape[0]
    return pl.pallas_call(
        functools.partial(_out_ln_kernel, tm=tm, rc=rc),
        grid=(n // tm,),
        in_specs=[
            pl.BlockSpec((tm, D_MODEL), lambda i: (i, 0)),
            pl.BlockSpec((D_MODEL, D_MODEL), lambda i: (0, 0)),
            pl.BlockSpec((tm, D_MODEL), lambda i: (i, 0)),
            pl.BlockSpec((1, D_MODEL), lambda i: (0, 0)),
            pl.BlockSpec((1, D_MODEL), lambda i: (0, 0)),
        ],
        out_specs=pl.BlockSpec((tm, D_MODEL), lambda i: (i, 0)),
        out_shape=jax.ShapeDtypeStruct((n, D_MODEL), F32),
        compiler_params=_cparams(("arbitrary",)),
    )(y2d, w, x2d, g.reshape(1, D_MODEL), b.reshape(1, D_MODEL))


Q_SCALE_LOG2 = math.log2(math.e) / math.sqrt(HEAD_DIM)


def _scale_q(w):
    return jnp.concatenate([w[:, :D_MODEL] * Q_SCALE_LOG2, w[:, D_MODEL:]], axis=1).astype(BF16)


def _layer_a(x, p, layer_idx):
    batch, seq, _ = x.shape
    x2d = x.reshape(batch * seq, D_MODEL)
    w_in = _scale_q(p["w_in"])
    planes = _project(x2d, w_in, seq=seq, dil=1, tile=1024)
    lam_init = 0.8 - 0.6 * math.exp(-0.3 * layer_idx)
    lamv = jnp.stack([p["lam_q1"], p["lam_k1"], p["lam_q2"], p["lam_k2"]]).astype(F32)
    y = _attn_a(planes, lamv, p["subln_g"].astype(F32).reshape(1, LANES), batch=batch, seq=seq, lam_init=lam_init)
    out = _out_ln(y.reshape(batch * seq, D_MODEL), p["w_out"].astype(BF16), x2d, p["ln_g"], p["ln_b"])
    return out.reshape(batch, seq, D_MODEL)


def _layer_b(x, p):
    batch, seq, _ = x.shape
    x2d = x.reshape(batch * seq, D_MODEL)
    w_in = p["w_in"]
    group_planes = []
    gate_planes = None
    for gi, (_, dil) in enumerate(B_GROUPS):
        w = w_in[:, 3 * gi * D_MODEL:3 * (gi + 1) * D_MODEL]
        if dil == 1:
            w = jnp.concatenate([w, w_in[:, 3 * len(B_GROUPS) * D_MODEL:]], axis=1)
        planes = _project(x2d, _scale_q(w), seq=seq, dil=dil, tile=TILE)
        if dil == 1:
            gate_planes = planes
        group_planes.append(planes)
    y = _attn_b(group_planes, gate_planes, 3 * SECTION_PLANES, batch=batch, seq=seq)
    out = _out_ln(y.reshape(batch * seq, D_MODEL), p["w_out"].astype(BF16), x2d, p["ln_g"], p["ln_b"])
    return out.reshape(batch, seq, D_MODEL)


def _trunk(x, layers):
    for i, p in enumerate(layers):
        x = _layer_a(x, p, i) if i % 2 == 0 else _layer_b(x, p)
    return x


def kernel(x_prompt, x_sample, w_in_0, lam_q1_0, lam_k1_0, lam_q2_0, lam_k2_0, subln_g_0, w_out_0, ln_g_0, ln_b_0, w_in_1, w_out_1, ln_g_1, ln_b_1, w_in_2, lam_q1_2, lam_k1_2, lam_q2_2, lam_k2_2, subln_g_2, w_out_2, ln_g_2, ln_b_2, w_in_3, w_out_3, ln_g_3, ln_b_3):
    layers = [
        dict(w_in=w_in_0, lam_q1=lam_q1_0, lam_k1=lam_k1_0, lam_q2=lam_q2_0, lam_k2=lam_k2_0,
             subln_g=subln_g_0, w_out=w_out_0, ln_g=ln_g_0, ln_b=ln_b_0),
        dict(w_in=w_in_1, w_out=w_out_1, ln_g=ln_g_1, ln_b=ln_b_1),
        dict(w_in=w_in_2, lam_q1=lam_q1_2, lam_k1=lam_k1_2, lam_q2=lam_q2_2, lam_k2=lam_k2_2,
             subln_g=subln_g_2, w_out=w_out_2, ln_g=ln_g_2, ln_b=ln_b_2),
        dict(w_in=w_in_3, w_out=w_out_3, ln_g=ln_g_3, ln_b=ln_b_3),
    ]
    return (_trunk(x_prompt, layers), _trunk(x_sample, layers))
```

```python
import functools
import math

import jax
import jax.numpy as jnp
from jax import lax
from jax.experimental import pallas as pl
from jax.experimental.pallas import tpu as pltpu

F32 = jnp.float32
BF16 = jnp.bfloat16

D_MODEL = 1024
DEPTH = 4
ROPE_THETA = 500000.0
LN_EPS = 1e-5
ALPHA = (2.0 * DEPTH) ** 0.25
HEAD_DIM = 64
LANES = 128
N_PAIRS = D_MODEL // LANES
SECTION_PLANES = D_MODEL // LANES
B_GROUPS = ((128, 1), (512, 4), (2048, 16))
RADIUS = 64
TILE = 2048
HALO = 64
QBLK = 128
MASKED = -1e30
GROUP = 4
VMEM_LIMIT = 56 * 1024 * 1024


def _cparams(sem, flags=None):
    return pltpu.CompilerParams(dimension_semantics=sem, vmem_limit_bytes=VMEM_LIMIT, flags=flags)


def _rope_tables(seq, dil, tm):
    half = HEAD_DIM // 8
    pos = jnp.arange(seq, dtype=jnp.int32).reshape(seq // tm, tm // dil, dil).transpose(0, 2, 1).reshape(seq)
    dim = jnp.arange(LANES, dtype=jnp.int32) % HEAD_DIM
    rotated = dim < 2 * half
    inv = jnp.where(rotated, ROPE_THETA ** (-(dim % half).astype(F32) / half), 0.0)
    ang = pos.astype(F32)[:, None] * inv[None, :]
    sin = jnp.sin(ang)
    c = jnp.cos(ang)
    sa = sin * jnp.where(dim < half, -1.0, 0.0)[None, :]
    sb = sin * jnp.where(rotated & (dim >= half), 1.0, 0.0)[None, :]
    return c, sa, sb


X_SLABS = D_MODEL // LANES


ROPE_COLS = 2 * D_MODEL


def _proj_kernel(*refs, dil, tm, sub_n, n_out):
    x_refs = refs[:X_SLABS]
    w_ref, c_ref, sa_ref, sb_ref, o_ref, xs_ref = refs[X_SLABS:]
    res_rows = tm // dil

    for c in range(X_SLABS):
        for r in range(dil):
            src = x_refs[c][pl.ds(r, res_rows, stride=dil), :] if dil > 1 else x_refs[c][...]
            xs_ref[r * res_rows:(r + 1) * res_rows, c * LANES:(c + 1) * LANES] = src.astype(BF16)

    for sd in range(n_out // sub_n):
        acc = jnp.dot(xs_ref[...], w_ref[:, sd * sub_n:(sd + 1) * sub_n], preferred_element_type=F32)
        for t in range(sub_n // LANES):
            a = acc[:, t * LANES:(t + 1) * LANES]
            if sd * sub_n < ROPE_COLS:
                a = a * c_ref[...] + pltpu.roll(a, LANES - 8, 1) * sa_ref[...] + pltpu.roll(a, 8, 1) * sb_ref[...]
            ab = a.astype(BF16)
            for r in range(dil):
                o_ref[sd * (sub_n // LANES) + t, r] = ab[r * res_rows:(r + 1) * res_rows]


def _project(x2d, w, *, seq, dil, tile, tm=1024, sub_n=512):
    n = x2d.shape[0]
    n_out = w.shape[1]
    per_tile = tile // tm
    n_seq_tiles = seq // tm
    tabs = _rope_tables(seq, dil, tm)
    kern = functools.partial(_proj_kernel, dil=dil, tm=tm, sub_n=sub_n, n_out=n_out)
    tab_spec = pl.BlockSpec((tm, LANES), lambda i: (i % n_seq_tiles, 0))
    x_specs = [pl.BlockSpec((tm, LANES), lambda i, c=c: (i, c)) for c in range(X_SLABS)]
    return pl.pallas_call(
        kern,
        grid=(n // tm,),
        in_specs=x_specs + [
            pl.BlockSpec((D_MODEL, n_out), lambda i: (0, 0)),
            tab_spec, tab_spec, tab_spec,
        ],
        out_specs=pl.BlockSpec((n_out // LANES, None, dil, tm // dil, LANES),
                               lambda i: (0, i // per_tile, 0, i % per_tile, 0)),
        out_shape=jax.ShapeDtypeStruct((n_out // LANES, n // tile, dil, tile // dil, LANES), BF16),
        scratch_shapes=[pltpu.VMEM((tm, D_MODEL), BF16)],
        compiler_params=_cparams(("arbitrary",)),
    )(*([x2d] * X_SLABS), w, *tabs)


N_SCORE_BUFS = 2


def _silu(g):
    return g * (1.0 / (1.0 + jnp.exp(-g)))


def _attn_a_kernel(lamv_ref, subg_ref, q_ref, k_ref, v_ref, g_ref, o_ref,
                   qt_ref, vt_ref, s0_ref, s1_ref, m_ref, l_ref, acc_ref, *, lam_init, bq, bk, nkv, nblk):
    @pl.when(pl.program_id(2) == 0)
    def _():
        for j in range(nkv):
            vt_ref[:, j * bk:(j + 1) * bk] = v_ref[j * bk:(j + 1) * bk, :].astype(F32).T.astype(BF16)

    for blk in range(nblk):
        qt = q_ref[blk * bq:(blk + 1) * bq, :].astype(F32).T
        dim = lax.broadcasted_iota(jnp.int32, qt.shape, 0)
        qt_ref[blk, :, 0:bq] = jnp.where(dim < HEAD_DIM, qt, 0.0).astype(BF16)
        qt_ref[blk, :, bq:2 * bq] = jnp.where(dim >= HEAD_DIM, qt, 0.0).astype(BF16)
    m_ref[...] = jnp.full(m_ref.shape, -jnp.inf, F32)
    l_ref[...] = jnp.zeros(l_ref.shape, F32)
    acc_ref[...] = jnp.zeros(acc_ref.shape, F32)

    def scores(blk, j, s_ref):
        kc = k_ref[j * bk:(j + 1) * bk, :]
        s_ref[...] = jnp.dot(kc, qt_ref[blk], preferred_element_type=F32)

    def update(blk, j, s_ref):
        m_prev = m_ref[blk]
        m_new = jnp.maximum(m_prev, jnp.max(s_ref[...], axis=0, keepdims=True))
        alpha = jnp.exp2(m_prev - m_new)
        p = jnp.exp2(s_ref[...] - m_new)
        l_ref[blk] = alpha * l_ref[blk] + jnp.sum(p, axis=0, keepdims=True)
        pv = jnp.dot(vt_ref[:, j * bk:(j + 1) * bk], p.astype(BF16), preferred_element_type=F32)
        acc_ref[blk] = alpha * acc_ref[blk] + pv
        m_ref[blk] = m_new

    def finalize(blk):
        lv = lamv_ref[...]
        lam = (jnp.exp(jnp.sum(lv[0:1] * lv[1:2], axis=1, keepdims=True))
               - jnp.exp(jnp.sum(lv[2:3] * lv[3:4], axis=1, keepdims=True)) + lam_init)
        ot = acc_ref[blk] / l_ref[blk]
        odt = ot[:, 0:bq] - lam * ot[:, bq:2 * bq]
        odt = odt * lax.rsqrt(jnp.mean(odt * odt, axis=0, keepdims=True) + LN_EPS)
        od = odt.T * subg_ref[...] * (1.0 - lam_init)
        rows = slice(blk * bq, (blk + 1) * bq)
        o_ref[rows, :] = (od * _silu(g_ref[rows, :].astype(F32))).astype(BF16)

    work = [(blk, j) for blk in range(nblk) for j in range(nkv)]
    bufs = (s0_ref, s1_ref)
    scores(*work[0], bufs[0])
    for i, (blk, j) in enumerate(work):
        if i + 1 < len(work):
            scores(*work[i + 1], bufs[(i + 1) % N_SCORE_BUFS])
        update(blk, j, bufs[i % N_SCORE_BUFS])
        if j == nkv - 1:
            finalize(blk)


def _attn_a(planes, lamv, subg, *, batch, seq, lam_init, bq=512, bk=1024, nblk=4):
    pl4 = planes.reshape(4 * SECTION_PLANES, batch, seq, LANES)
    nkv = seq // bk
    rows = nblk * bq
    assert seq % bk == 0 and seq % rows == 0
    kern = functools.partial(_attn_a_kernel, lam_init=lam_init, bq=bq, bk=bk, nkv=nkv, nblk=nblk)

    def q_rows(s):
        return pl.BlockSpec((None, None, rows, LANES), lambda b, h, qi: (s * SECTION_PLANES + h, b, qi, 0))

    def all_rows(s):
        return pl.BlockSpec((None, None, seq, LANES), lambda b, h, qi: (s * SECTION_PLANES + h, b, 0, 0))

    return pl.pallas_call(
        kern,
        grid=(batch, N_PAIRS, seq // rows),
        in_specs=[
            pl.BlockSpec((4, HEAD_DIM), lambda b, h, qi: (0, 0)),
            pl.BlockSpec((1, LANES), lambda b, h, qi: (0, 0)),
            q_rows(0), all_rows(1), all_rows(2), q_rows(3),
        ],
        out_specs=pl.BlockSpec((None, rows, LANES), lambda b, h, qi: (b, qi, h)),
        out_shape=jax.ShapeDtypeStruct((batch, seq, D_MODEL), BF16),
        scratch_shapes=[
            pltpu.VMEM((nblk, LANES, 2 * bq), BF16),
            pltpu.VMEM((LANES, seq), BF16),
        ] + [pltpu.VMEM((bk, 2 * bq), F32)] * N_SCORE_BUFS + [
            pltpu.VMEM((nblk, 1, 2 * bq), F32),
            pltpu.VMEM((nblk, 1, 2 * bq), F32),
            pltpu.VMEM((nblk, LANES, 2 * bq), F32),
        ],
        compiler_params=_cparams(("arbitrary", "arbitrary", "arbitrary")),
    )(lamv, subg, pl4, pl4, pl4, pl4)


def _attn_b_group(gi, dil, seq, q_ref, k_ref, kp_ref, kn_ref, v_ref, vp_ref, vn_ref,
                  kw_ref, vw_ref, og_ref, mg_ref, lg_ref, bias_ref):
    rows = TILE // dil
    nb64 = rows // HALO
    tile_i = pl.program_id(2)
    length = seq // dil

    kw_ref[0:dil, 0:HALO, :] = kp_ref[...]
    vw_ref[0:dil, 0:HALO, :] = vp_ref[...]
    for n in range(nb64):
        kw_ref[0:dil, HALO + n * HALO:2 * HALO + n * HALO, :] = k_ref[:, n]
        vw_ref[0:dil, HALO + n * HALO:2 * HALO + n * HALO, :] = v_ref[:, n]
    kw_ref[0:dil, HALO + rows:2 * HALO + rows, :] = kn_ref[...]
    vw_ref[0:dil, HALO + rows:2 * HALO + rows, :] = vn_ref[...]

    win = QBLK + 2 * HALO
    lane_q = lax.broadcasted_iota(jnp.int32, (QBLK, LANES), 1)
    lo_q = lane_q < HEAD_DIM
    blocks = rows // QBLK
    unroll = max(1, 8 // blocks)

    def block_scores(r, blk):
        q2 = q_ref[r, 2 * blk:2 * blk + 2].reshape(QBLK, LANES)
        zero = jnp.zeros_like(q2)
        qs = jnp.concatenate([jnp.where(lo_q, q2, zero), jnp.where(lo_q, zero, q2)], axis=0)
        kwin = kw_ref[r, blk * QBLK:blk * QBLK + win, :]
        s = lax.dot_general(qs, kwin, (((1,), (1,)), ((), ())), preferred_element_type=F32)
        first = tile_i * rows + blk * QBLK
        edge = (first == 0).astype(jnp.int32) + 2 * (first + QBLK == length).astype(jnp.int32)
        return s + bias_ref[edge]

    def block_softmax(s):
        m = jnp.max(s, axis=1, keepdims=True)
        p = jnp.exp2(s - m)
        return p.astype(BF16), m, jnp.sum(p, axis=1, keepdims=True)

    def per_head(a):
        return jnp.where(lo_q, jnp.broadcast_to(a[0:QBLK], (QBLK, LANES)),
                         jnp.broadcast_to(a[QBLK:2 * QBLK], (QBLK, LANES)))

    def block_output(r, blk, pb, m, l):
        vwin = vw_ref[r, blk * QBLK:blk * QBLK + win, :]
        o0 = jnp.dot(pb[0:QBLK], vwin, preferred_element_type=F32)
        o1 = jnp.dot(pb[QBLK:2 * QBLK], vwin, preferred_element_type=F32)
        tok = pl.ds(r + blk * QBLK * dil, QBLK, stride=dil) if dil > 1 else pl.ds(blk * QBLK, QBLK)
        og_ref[gi, tok, :] = jnp.where(lo_q, o0, o1)
        mg_ref[gi, tok, :] = per_head(m)
        lg_ref[gi, tok, :] = per_head(l)

    def body(it, carry):
        items = [(it * unroll + u, blk) for u in range(unroll) for blk in range(blocks)]
        for g0 in range(0, len(items), GROUP):
            group = items[g0:g0 + GROUP]
            scores = [block_scores(r, blk) for r, blk in group]
            soft = [block_softmax(s) for s in scores]
            for (r, blk), (pb, m, l) in zip(group, soft):
                block_output(r, blk, pb, m, l)
        return carry

    if dil == 1:
        body(0, 0)
    else:
        lax.fori_loop(0, dil // unroll, body, 0)


def _attn_b_kernel(*refs, seq):
    n_in = 7 * len(B_GROUPS) + 1
    gate_ref = refs[n_in - 1]
    o_ref = refs[n_in]
    kw_ref, vw_ref, og_ref, mg_ref, lg_ref, bias_ref = refs[n_in + 1:]

    @pl.when((pl.program_id(0) == 0) & (pl.program_id(1) == 0) & (pl.program_id(2) == 0))
    def _():
        win = QBLK + 2 * HALO
        row_i = lax.broadcasted_iota(jnp.int32, (2 * QBLK, win), 0)
        col_i = lax.broadcasted_iota(jnp.int32, (2 * QBLK, win), 1)
        qpos = jnp.where(row_i >= QBLK, row_i - QBLK, row_i)
        band = jnp.where(jnp.abs(col_i - HALO - qpos) <= RADIUS, 0.0, MASKED)
        low = jnp.where(col_i < HALO, MASKED, 0.0)
        high = jnp.where(col_i >= QBLK + HALO, MASKED, 0.0)
        bias_ref[0] = band
        bias_ref[1] = band + low
        bias_ref[2] = band + high
        bias_ref[3] = band + low + high

    for gi, (_, dil) in enumerate(B_GROUPS):
        _attn_b_group(gi, dil, seq, *refs[7 * gi:7 * gi + 7], kw_ref, vw_ref, og_ref, mg_ref, lg_ref, bias_ref)
    m0, m1, m2 = mg_ref[0], mg_ref[1], mg_ref[2]
    mx = jnp.maximum(jnp.maximum(m0, m1), m2)
    e0, e1, e2 = jnp.exp2(m0 - mx), jnp.exp2(m1 - mx), jnp.exp2(m2 - mx)
    den = e0 * lg_ref[0] + e1 * lg_ref[1] + e2 * lg_ref[2]
    o = (e0 * og_ref[0] + e1 * og_ref[1] + e2 * og_ref[2]) / den
    o_ref[...] = (o * _silu(gate_ref[...].astype(F32))).astype(BF16)


def _attn_b(group_planes, gate_planes, gate_base, *, batch, seq):
    n_tiles = seq // TILE
    in_specs, args = [], []
    for (_, dil), planes in zip(B_GROUPS, group_planes):
        rows = TILE // dil
        nb64 = rows // HALO
        view = planes.reshape(planes.shape[0], batch, n_tiles, dil, nb64, HALO, LANES)

        def main(s):
            return pl.BlockSpec((None, None, None, dil, nb64, HALO, LANES),
                                lambda b, h, i, s=s: (s * SECTION_PLANES + h, b, i, 0, 0, 0, 0))

        def prev(s, nb64=nb64):
            return pl.BlockSpec((None, None, None, dil, None, HALO, LANES),
                                lambda b, h, i, s=s: (s * SECTION_PLANES + h, b, jnp.maximum(i - 1, 0), 0, nb64 - 1, 0, 0))

        def nxt(s):
            return pl.BlockSpec((None, None, None, dil, None, HALO, LANES),
                                lambda b, h, i, s=s: (s * SECTION_PLANES + h, b, jnp.minimum(i + 1, n_tiles - 1), 0, 0, 0, 0))

        in_specs += [main(0), main(1), prev(1), nxt(1), main(2), prev(2), nxt(2)]
        args += [view] * 7
    gate_view = gate_planes.reshape(gate_planes.shape[0], batch, seq, LANES)
    in_specs.append(pl.BlockSpec((None, None, TILE, LANES), lambda b, h, i: (gate_base + h, b, i, 0)))
    args.append(gate_view)
    max_dil = max(d for _, d in B_GROUPS)
    return pl.pallas_call(
        functools.partial(_attn_b_kernel, seq=seq),
        grid=(batch, N_PAIRS, n_tiles),
        in_specs=in_specs,
        out_specs=pl.BlockSpec((None, TILE, LANES), lambda b, h, i: (b, i, h)),
        out_shape=jax.ShapeDtypeStruct((batch, seq, D_MODEL), BF16),
        scratch_shapes=[
            pltpu.VMEM((max_dil, TILE + 2 * HALO, LANES), BF16),
            pltpu.VMEM((max_dil, TILE + 2 * HALO, LANES), BF16),
            pltpu.VMEM((len(B_GROUPS), TILE, LANES), F32),
            pltpu.VMEM((len(B_GROUPS), TILE, LANES), F32),
            pltpu.VMEM((len(B_GROUPS), TILE, LANES), F32),
            pltpu.VMEM((4, 2 * QBLK, QBLK + 2 * HALO), F32),
        ],
        compiler_params=_cparams(("arbitrary", "arbitrary", "arbitrary")),
    )(*args)


def _out_ln_kernel(y_ref, w_ref, x_ref, g_ref, b_ref, o_ref, *, tm, rc):
    for r0 in range(0, tm, rc):
        rows = slice(r0, r0 + rc)
        f = jnp.dot(y_ref[rows, :], w_ref[...], preferred_element_type=F32)
        z = ALPHA * x_ref[rows, :] + f
        mu = jnp.mean(z, axis=1, keepdims=True)
        zc = z - mu
        var = jnp.mean(zc * zc, axis=1, keepdims=True)
        o_ref[rows, :] = zc * lax.rsqrt(var + LN_EPS) * g_ref[...] + b_ref[...]


def _out_ln(y2d, w, x2d, g, b, *, tm=1024, rc=512):
    n = x2d.shape[0]
    return pl.pallas_call(
        functools.partial(_out_ln_kernel, tm=tm, rc=rc),
        grid=(n // tm,),
        in_specs=[
            pl.BlockSpec((tm, D_MODEL), lambda i: (i, 0)),
            pl.BlockSpec((D_MODEL, D_MODEL), lambda i: (0, 0)),
            pl.BlockSpec((tm, D_MODEL), lambda i: (i, 0)),
            pl.BlockSpec((1, D_MODEL), lambda i: (0, 0)),
            pl.BlockSpec((1, D_MODEL), lambda i: (0, 0)),
        ],
        out_specs=pl.BlockSpec((tm, D_MODEL), lambda i: (i, 0)),
        out_shape=jax.ShapeDtypeStruct((n, D_MODEL), F32),
        compiler_params=_cparams(("arbitrary",)),
    )(y2d, w, x2d, g.reshape(1, D_MODEL), b.reshape(1, D_MODEL))


Q_SCALE_LOG2 = math.log2(math.e) / math.sqrt(HEAD_DIM)


def _scale_q(w):
    return jnp.concatenate([w[:, :D_MODEL] * Q_SCALE_LOG2, w[:, D_MODEL:]], axis=1).astype(BF16)


def _layer_a(x, p, layer_idx):
    batch, seq, _ = x.shape
    x2d = x.reshape(batch * seq, D_MODEL)
    w_in = _scale_q(p["w_in"])
    planes = _project(x2d, w_in, seq=seq, dil=1, tile=1024)
    lam_init = 0.8 - 0.6 * math.exp(-0.3 * layer_idx)
    lamv = jnp.stack([p["lam_q1"], p["lam_k1"], p["lam_q2"], p["lam_k2"]]).astype(F32)
    y = _attn_a(planes, lamv, p["subln_g"].astype(F32).reshape(1, LANES), batch=batch, seq=seq, lam_init=lam_init)
    out = _out_ln(y.reshape(batch * seq, D_MODEL), p["w_out"].astype(BF16), x2d, p["ln_g"], p["ln_b"])
    return out.reshape(batch, seq, D_MODEL)


def _layer_b(x, p):
    batch, seq, _ = x.shape
    x2d = x.reshape(batch * seq, D_MODEL)
    w_in = p["w_in"]
    group_planes = []
    gate_planes = None
    for gi, (_, dil) in enumerate(B_GROUPS):
        w = w_in[:, 3 * gi * D_MODEL:3 * (gi + 1) * D_MODEL]
        if dil == 1:
            w = jnp.concatenate([w, w_in[:, 3 * len(B_GROUPS) * D_MODEL:]], axis=1)
        planes = _project(x2d, _scale_q(w), seq=seq, dil=dil, tile=TILE)
        if dil == 1:
            gate_planes = planes
        group_planes.append(planes)
    y = _attn_b(group_planes, gate_planes, 3 * SECTION_PLANES, batch=batch, seq=seq)
    out = _out_ln(y.reshape(batch * seq, D_MODEL), p["w_out"].astype(BF16), x2d, p["ln_g"], p["ln_b"])
    return out.reshape(batch, seq, D_MODEL)


def _trunk(x, layers):
    for i, p in enumerate(layers):
        x = _layer_a(x, p, i) if i % 2 == 0 else _layer_b(x, p)
    return x


def kernel(x_prompt, x_sample, w_in_0, lam_q1_0, lam_k1_0, lam_q2_0, lam_k2_0, subln_g_0, w_out_0, ln_g_0, ln_b_0, w_in_1, w_out_1, ln_g_1, ln_b_1, w_in_2, lam_q1_2, lam_k1_2, lam_q2_2, lam_k2_2, subln_g_2, w_out_2, ln_g_2, ln_b_2, w_in_3, w_out_3, ln_g_3, ln_b_3):
    layers = [
        dict(w_in=w_in_0, lam_q1=lam_q1_0, lam_k1=lam_k1_0, lam_q2=lam_q2_0, lam_k2=lam_k2_0,
             subln_g=subln_g_0, w_out=w_out_0, ln_g=ln_g_0, ln_b=ln_b_0),
        dict(w_in=w_in_1, w_out=w_out_1, ln_g=ln_g_1, ln_b=ln_b_1),
        dict(w_in=w_in_2, lam_q1=lam_q1_2, lam_k1=lam_k1_2, lam_q2=lam_q2_2, lam_k2=lam_k2_2,
             subln_g=subln_g_2, w_out=w_out_2, ln_g=ln_g_2, ln_b=ln_b_2),
        dict(w_in=w_in_3, w_out=w_out_3, ln_g=ln_g_3, ln_b=ln_b_3),
    ]
    return (_trunk(x_prompt, layers), _trunk(x_sample, layers))
```

```python
import functools
import math

import jax
import jax.numpy as jnp
from jax import lax
from jax.experimental import pallas as pl
from jax.experimental.pallas import tpu as pltpu

F32 = jnp.float32
BF16 = jnp.bfloat16

D_MODEL = 1024
DEPTH = 4
ROPE_THETA = 500000.0
LN_EPS = 1e-5
ALPHA = (2.0 * DEPTH) ** 0.25
HEAD_DIM = 64
LANES = 128
N_PAIRS = D_MODEL // LANES
SECTION_PLANES = D_MODEL // LANES
B_GROUPS = ((128, 1), (512, 4), (2048, 16))
RADIUS = 64
TILE = 2048
HALO = 64
QBLK = 128
MASKED = -1e30
GROUP = 4
VMEM_LIMIT = 56 * 1024 * 1024


def _cparams(sem, flags=None):
    return pltpu.CompilerParams(dimension_semantics=sem, vmem_limit_bytes=VMEM_LIMIT, flags=flags)


def _rope_tables(seq, dil, tm):
    half = HEAD_DIM // 8
    pos = jnp.arange(seq, dtype=jnp.int32).reshape(seq // tm, tm // dil, dil).transpose(0, 2, 1).reshape(seq)
    dim = jnp.arange(LANES, dtype=jnp.int32) % HEAD_DIM
    rotated = dim < 2 * half
    inv = jnp.where(rotated, ROPE_THETA ** (-(dim % half).astype(F32) / half), 0.0)
    ang = pos.astype(F32)[:, None] * inv[None, :]
    sin = jnp.sin(ang)
    c = jnp.cos(ang)
    sa = sin * jnp.where(dim < half, -1.0, 0.0)[None, :]
    sb = sin * jnp.where(rotated & (dim >= half), 1.0, 0.0)[None, :]
    return c, sa, sb


X_SLABS = D_MODEL // LANES


ROPE_COLS = 2 * D_MODEL


def _proj_kernel(*refs, dil, tm, sub_n, n_out):
    x_refs = refs[:X_SLABS]
    w_ref, c_ref, sa_ref, sb_ref, o_ref, xs_ref = refs[X_SLABS:]
    res_rows = tm // dil

    for c in range(X_SLABS):
        for r in range(dil):
            src = x_refs[c][pl.ds(r, res_rows, stride=dil), :] if dil > 1 else x_refs[c][...]
            xs_ref[r * res_rows:(r + 1) * res_rows, c * LANES:(c + 1) * LANES] = src.astype(BF16)

    for sd in range(n_out // sub_n):
        acc = jnp.dot(xs_ref[...], w_ref[:, sd * sub_n:(sd + 1) * sub_n], preferred_element_type=F32)
        for t in range(sub_n // LANES):
            a = acc[:, t * LANES:(t + 1) * LANES]
            if sd * sub_n < ROPE_COLS:
                a = a * c_ref[...] + pltpu.roll(a, LANES - 8, 1) * sa_ref[...] + pltpu.roll(a, 8, 1) * sb_ref[...]
            ab = a.astype(BF16)
            for r in range(dil):
                o_ref[sd * (sub_n // LANES) + t, r] = ab[r * res_rows:(r + 1) * res_rows]


def _project(x2d, w, *, seq, dil, tile, tm=1024, sub_n=512):
    n = x2d.shape[0]
    n_out = w.shape[1]
    per_tile = tile // tm
    n_seq_tiles = seq // tm
    tabs = _rope_tables(seq, dil, tm)
    kern = functools.partial(_proj_kernel, dil=dil, tm=tm, sub_n=sub_n, n_out=n_out)
    tab_spec = pl.BlockSpec((tm, LANES), lambda i: (i % n_seq_tiles, 0))
    x_specs = [pl.BlockSpec((tm, LANES), lambda i, c=c: (i, c)) for c in range(X_SLABS)]
    return pl.pallas_call(
        kern,
        grid=(n // tm,),
        in_specs=x_specs + [
            pl.BlockSpec((D_MODEL, n_out), lambda i: (0, 0)),
            tab_spec, tab_spec, tab_spec,
        ],
        out_specs=pl.BlockSpec((n_out // LANES, None, dil, tm // dil, LANES),
                               lambda i: (0, i // per_tile, 0, i % per_tile, 0)),
        out_shape=jax.ShapeDtypeStruct((n_out // LANES, n // tile, dil, tile // dil, LANES), BF16),
        scratch_shapes=[pltpu.VMEM((tm, D_MODEL), BF16)],
        compiler_params=_cparams(("arbitrary",)),
    )(*([x2d] * X_SLABS), w, *tabs)


N_SCORE_BUFS = 2


def _silu(g):
    return g * (1.0 / (1.0 + jnp.exp(-g)))


def _attn_a_kernel(lamv_ref, subg_ref, q_ref, k_ref, v_ref, g_ref, o_ref,
                   qt_ref, vt_ref, s0_ref, s1_ref, m_ref, l_ref, acc_ref, *, lam_init, bq, bk, nkv, nblk):
    @pl.when(pl.program_id(2) == 0)
    def _():
        for j in range(nkv):
            vt_ref[:, j * bk:(j + 1) * bk] = v_ref[j * bk:(j + 1) * bk, :].astype(F32).T.astype(BF16)

    for blk in range(nblk):
        qt = q_ref[blk * bq:(blk + 1) * bq, :].astype(F32).T
        dim = lax.broadcasted_iota(jnp.int32, qt.shape, 0)
        qt_ref[blk, :, 0:bq] = jnp.where(dim < HEAD_DIM, qt, 0.0).astype(BF16)
        qt_ref[blk, :, bq:2 * bq] = jnp.where(dim >= HEAD_DIM, qt, 0.0).astype(BF16)
    m_ref[...] = jnp.full(m_ref.shape, -jnp.inf, F32)
    l_ref[...] = jnp.zeros(l_ref.shape, F32)
    acc_ref[...] = jnp.zeros(acc_ref.shape, F32)

    def scores(blk, j, s_ref):
        kc = k_ref[j * bk:(j + 1) * bk, :]
        s_ref[...] = jnp.dot(kc, qt_ref[blk], preferred_element_type=F32)

    def update(blk, j, s_ref):
        m_prev = m_ref[blk]
        m_new = jnp.maximum(m_prev, jnp.max(s_ref[...], axis=0, keepdims=True))
        alpha = jnp.exp2(m_prev - m_new)
        p = jnp.exp2(s_ref[...] - m_new)
        l_ref[blk] = alpha * l_ref[blk] + jnp.sum(p, axis=0, keepdims=True)
        pv = jnp.dot(vt_ref[:, j * bk:(j + 1) * bk], p.astype(BF16), preferred_element_type=F32)
        acc_ref[blk] = alpha * acc_ref[blk] + pv
        m_ref[blk] = m_new

    def finalize(blk):
        lv = lamv_ref[...]
        lam = (jnp.exp(jnp.sum(lv[0:1] * lv[1:2], axis=1, keepdims=True))
               - jnp.exp(jnp.sum(lv[2:3] * lv[3:4], axis=1, keepdims=True)) + lam_init)
        ot = acc_ref[blk] / l_ref[blk]
        odt = ot[:, 0:bq] - lam * ot[:, bq:2 * bq]
        odt = odt * lax.rsqrt(jnp.mean(odt * odt, axis=0, keepdims=True) + LN_EPS)
        od = odt.T * subg_ref[...] * (1.0 - lam_init)
        rows = slice(blk * bq, (blk + 1) * bq)
        o_ref[rows, :] = (od * _silu(g_ref[rows, :].astype(F32))).astype(BF16)

    work = [(blk, j) for blk in range(nblk) for j in range(nkv)]
    bufs = (s0_ref, s1_ref)
    scores(*work[0], bufs[0])
    for i, (blk, j) in enumerate(work):
        if i + 1 < len(work):
            scores(*work[i + 1], bufs[(i + 1) % N_SCORE_BUFS])
        update(blk, j, bufs[i % N_SCORE_BUFS])
        if j == nkv - 1:
            finalize(blk)


def _attn_a(planes, lamv, subg, *, batch, seq, lam_init, bq=512, bk=1024, nblk=8):
    pl4 = planes.reshape(4 * SECTION_PLANES, batch, seq, LANES)
    nkv = seq // bk
    rows = nblk * bq
    assert seq % bk == 0 and seq % rows == 0
    kern = functools.partial(_attn_a_kernel, lam_init=lam_init, bq=bq, bk=bk, nkv=nkv, nblk=nblk)

    def q_rows(s):
        return pl.BlockSpec((None, None, rows, LANES), lambda b, h, qi: (s * SECTION_PLANES + h, b, qi, 0))

    def all_rows(s):
        return pl.BlockSpec((None, None, seq, LANES), lambda b, h, qi: (s * SECTION_PLANES + h, b, 0, 0))

    return pl.pallas_call(
        kern,
        grid=(batch, N_PAIRS, seq // rows),
        in_specs=[
            pl.BlockSpec((4, HEAD_DIM), lambda b, h, qi: (0, 0)),
            pl.BlockSpec((1, LANES), lambda b, h, qi: (0, 0)),
            q_rows(0), all_rows(1), all_rows(2), q_rows(3),
        ],
        out_specs=pl.BlockSpec((None, rows, LANES), lambda b, h, qi: (b, qi, h)),
        out_shape=jax.ShapeDtypeStruct((batch, seq, D_MODEL), BF16),
        scratch_shapes=[
            pltpu.VMEM((nblk, LANES, 2 * bq), BF16),
            pltpu.VMEM((LANES, seq), BF16),
        ] + [pltpu.VMEM((bk, 2 * bq), F32)] * N_SCORE_BUFS + [
            pltpu.VMEM((nblk, 1, 2 * bq), F32),
            pltpu.VMEM((nblk, 1, 2 * bq), F32),
            pltpu.VMEM((nblk, LANES, 2 * bq), F32),
        ],
        compiler_params=_cparams(("arbitrary", "arbitrary", "arbitrary")),
    )(lamv, subg, pl4, pl4, pl4, pl4)


def _attn_b_group(gi, dil, seq, q_ref, k_ref, kp_ref, kn_ref, v_ref, vp_ref, vn_ref,
                  kw_ref, vw_ref, og_ref, mg_ref, lg_ref, bias_ref):
    rows = TILE // dil
    nb64 = rows // HALO
    tile_i = pl.program_id(2)
    length = seq // dil

    kw_ref[0:dil, 0:HALO, :] = kp_ref[...]
    vw_ref[0:dil, 0:HALO, :] = vp_ref[...]
    for n in range(nb64):
        kw_ref[0:dil, HALO + n * HALO:2 * HALO + n * HALO, :] = k_ref[:, n]
        vw_ref[0:dil, HALO + n * HALO:2 * HALO + n * HALO, :] = v_ref[:, n]
    kw_ref[0:dil, HALO + rows:2 * HALO + rows, :] = kn_ref[...]
    vw_ref[0:dil, HALO + rows:2 * HALO + rows, :] = vn_ref[...]

    win = QBLK + 2 * HALO
    lane_q = lax.broadcasted_iota(jnp.int32, (QBLK, LANES), 1)
    lo_q = lane_q < HEAD_DIM
    blocks = rows // QBLK
    unroll = max(1, 8 // blocks)

    def block_scores(r, blk):
        q2 = q_ref[r, 2 * blk:2 * blk + 2].reshape(QBLK, LANES)
        zero = jnp.zeros_like(q2)
        qs = jnp.concatenate([jnp.where(lo_q, q2, zero), jnp.where(lo_q, zero, q2)], axis=0)
        kwin = kw_ref[r, blk * QBLK:blk * QBLK + win, :]
        s = lax.dot_general(qs, kwin, (((1,), (1,)), ((), ())), preferred_element_type=F32)
        first = tile_i * rows + blk * QBLK
        edge = (first == 0).astype(jnp.int32) + 2 * (first + QBLK == length).astype(jnp.int32)
        return s + bias_ref[edge]

    def block_softmax(s):
        m = jnp.max(s, axis=1, keepdims=True)
        p = jnp.exp2(s - m)
        return p.astype(BF16), m, jnp.sum(p, axis=1, keepdims=True)

    def per_head(a):
        return jnp.where(lo_q, jnp.broadcast_to(a[0:QBLK], (QBLK, LANES)),
                         jnp.broadcast_to(a[QBLK:2 * QBLK], (QBLK, LANES)))

    def block_output(r, blk, pb, m, l):
        vwin = vw_ref[r, blk * QBLK:blk * QBLK + win, :]
        o0 = jnp.dot(pb[0:QBLK], vwin, preferred_element_type=F32)
        o1 = jnp.dot(pb[QBLK:2 * QBLK], vwin, preferred_element_type=F32)
        tok = pl.ds(r + blk * QBLK * dil, QBLK, stride=dil) if dil > 1 else pl.ds(blk * QBLK, QBLK)
        og_ref[gi, tok, :] = jnp.where(lo_q, o0, o1)
        mg_ref[gi, tok, :] = per_head(m)
        lg_ref[gi, tok, :] = per_head(l)

    def body(it, carry):
        items = [(it * unroll + u, blk) for u in range(unroll) for blk in range(blocks)]
        for g0 in range(0, len(items), GROUP):
            group = items[g0:g0 + GROUP]
            scores = [block_scores(r, blk) for r, blk in group]
            soft = [block_softmax(s) for s in scores]
            for (r, blk), (pb, m, l) in zip(group, soft):
                block_output(r, blk, pb, m, l)
        return carry

    if dil == 1:
        body(0, 0)
    else:
        lax.fori_loop(0, dil // unroll, body, 0)


def _attn_b_kernel(*refs, seq):
    n_in = 7 * len(B_GROUPS) + 1
    gate_ref = refs[n_in - 1]
    o_ref = refs[n_in]
    kw_ref, vw_ref, og_ref, mg_ref, lg_ref, bias_ref = refs[n_in + 1:]

    @pl.when((pl.program_id(0) == 0) & (pl.program_id(1) == 0) & (pl.program_id(2) == 0))
    def _():
        win = QBLK + 2 * HALO
        row_i = lax.broadcasted_iota(jnp.int32, (2 * QBLK, win), 0)
        col_i = lax.broadcasted_iota(jnp.int32, (2 * QBLK, win), 1)
        qpos = jnp.where(row_i >= QBLK, row_i - QBLK, row_i)
        band = jnp.where(jnp.abs(col_i - HALO - qpos) <= RADIUS, 0.0, MASKED)
        low = jnp.where(col_i < HALO, MASKED, 0.0)
        high = jnp.where(col_i >= QBLK + HALO, MASKED, 0.0)
        bias_ref[0] = band
        bias_ref[1] = band + low
        bias_ref[2] = band + high
        bias_ref[3] = band + low + high

    for gi, (_, dil) in enumerate(B_GROUPS):
        _attn_b_group(gi, dil, seq, *refs[7 * gi:7 * gi + 7], kw_ref, vw_ref, og_ref, mg_ref, lg_ref, bias_ref)
    m0, m1, m2 = mg_ref[0], mg_ref[1], mg_ref[2]
    mx = jnp.maximum(jnp.maximum(m0, m1), m2)
    e0, e1, e2 = jnp.exp2(m0 - mx), jnp.exp2(m1 - mx), jnp.exp2(m2 - mx)
    den = e0 * lg_ref[0] + e1 * lg_ref[1] + e2 * lg_ref[2]
    o = (e0 * og_ref[0] + e1 * og_ref[1] + e2 * og_ref[2]) / den
    o_ref[...] = (o * _silu(gate_ref[...].astype(F32))).astype(BF16)


def _attn_b(group_planes, gate_planes, gate_base, *, batch, seq):
    n_tiles = seq // TILE
    in_specs, args = [], []
    for (_, dil), planes in zip(B_GROUPS, group_planes):
        rows = TILE // dil
        nb64 = rows // HALO
        view = planes.reshape(planes.shape[0], batch, n_tiles, dil, nb64, HALO, LANES)

        def main(s):
            return pl.BlockSpec((None, None, None, dil, nb64, HALO, LANES),
                                lambda b, h, i, s=s: (s * SECTION_PLANES + h, b, i, 0, 0, 0, 0))

        def prev(s, nb64=nb64):
            return pl.BlockSpec((None, None, None, dil, None, HALO, LANES),
                                lambda b, h, i, s=s: (s * SECTION_PLANES + h, b, jnp.maximum(i - 1, 0), 0, nb64 - 1, 0, 0))

        def nxt(s):
            return pl.BlockSpec((None, None, None, dil, None, HALO, LANES),
                                lambda b, h, i, s=s: (s * SECTION_PLANES + h, b, jnp.minimum(i + 1, n_tiles - 1), 0, 0, 0, 0))

        in_specs += [main(0), main(1), prev(1), nxt(1), main(2), prev(2), nxt(2)]
        args += [view] * 7
    gate_view = gate_planes.reshape(gate_planes.shape[0], batch, seq, LANES)
    in_specs.append(pl.BlockSpec((None, None, TILE, LANES), lambda b, h, i: (gate_base + h, b, i, 0)))
    args.append(gate_view)
    max_dil = max(d for _, d in B_GROUPS)
    return pl.pallas_call(
        functools.partial(_attn_b_kernel, seq=seq),
        grid=(batch, N_PAIRS, n_tiles),
        in_specs=in_specs,
        out_specs=pl.BlockSpec((None, TILE, LANES), lambda b, h, i: (b, i, h)),
        out_shape=jax.ShapeDtypeStruct((batch, seq, D_MODEL), BF16),
        scratch_shapes=[
            pltpu.VMEM((max_dil, TILE + 2 * HALO, LANES), BF16),
            pltpu.VMEM((max_dil, TILE + 2 * HALO, LANES), BF16),
            pltpu.VMEM((len(B_GROUPS), TILE, LANES), F32),
            pltpu.VMEM((len(B_GROUPS), TILE, LANES), F32),
            pltpu.VMEM((len(B_GROUPS), TILE, LANES), F32),
            pltpu.VMEM((4, 2 * QBLK, QBLK + 2 * HALO), F32),
        ],
        compiler_params=_cparams(("arbitrary", "arbitrary", "arbitrary")),
    )(*args)


def _out_ln_kernel(y_ref, w_ref, x_ref, g_ref, b_ref, o_ref, *, tm, rc):
    for r0 in range(0, tm, rc):
        rows = slice(r0, r0 + rc)
        f = jnp.dot(y_ref[rows, :], w_ref[...], preferred_element_type=F32)
        z = ALPHA * x_ref[rows, :] + f
        mu = jnp.mean(z, axis=1, keepdims=True)
        zc = z - mu
        var = jnp.mean(zc * zc, axis=1, keepdims=True)
        o_ref[rows, :] = zc * lax.rsqrt(var + LN_EPS) * g_ref[...] + b_ref[...]


def _out_ln(y2d, w, x2d, g, b, *, tm=1024, rc=512):
    n = x2d.shape[0]
    return pl.pallas_call(
        functools.partial(_out_ln_kernel, tm=tm, rc=rc),
        grid=(n // tm,),
        in_specs=[
            pl.BlockSpec((tm, D_MODEL), lambda i: (i, 0)),
            pl.BlockSpec((D_MODEL, D_MODEL), lambda i: (0, 0)),
            pl.BlockSpec((tm, D_MODEL), lambda i: (i, 0)),
            pl.BlockSpec((1, D_MODEL), lambda i: (0, 0)),
            pl.BlockSpec((1, D_MODEL), lambda i: (0, 0)),
        ],
        out_specs=pl.BlockSpec((tm, D_MODEL), lambda i: (i, 0)),
        out_shape=jax.ShapeDtypeStruct((n, D_MODEL), F32),
        compiler_params=_cparams(("arbitrary",)),
    )(y2d, w, x2d, g.reshape(1, D_MODEL), b.reshape(1, D_MODEL))


Q_SCALE_LOG2 = math.log2(math.e) / math.sqrt(HEAD_DIM)


def _scale_q(w):
    return jnp.concatenate([w[:, :D_MODEL] * Q_SCALE_LOG2, w[:, D_MODEL:]], axis=1).astype(BF16)


def _layer_a(x, p, layer_idx):
    batch, seq, _ = x.shape
    x2d = x.reshape(batch * seq, D_MODEL)
    w_in = _scale_q(p["w_in"])
    planes = _project(x2d, w_in, seq=seq, dil=1, tile=1024)
    lam_init = 0.8 - 0.6 * math.exp(-0.3 * layer_idx)
    lamv = jnp.stack([p["lam_q1"], p["lam_k1"], p["lam_q2"], p["lam_k2"]]).astype(F32)
    y = _attn_a(planes, lamv, p["subln_g"].astype(F32).reshape(1, LANES), batch=batch, seq=seq, lam_init=lam_init)
    out = _out_ln(y.reshape(batch * seq, D_MODEL), p["w_out"].astype(BF16), x2d, p["ln_g"], p["ln_b"])
    return out.reshape(batch, seq, D_MODEL)


def _layer_b(x, p):
    batch, seq, _ = x.shape
    x2d = x.reshape(batch * seq, D_MODEL)
    w_in = p["w_in"]
    group_planes = []
    gate_planes = None
    for gi, (_, dil) in enumerate(B_GROUPS):
        w = w_in[:, 3 * gi * D_MODEL:3 * (gi + 1) * D_MODEL]
        if dil == 1:
            w = jnp.concatenate([w, w_in[:, 3 * len(B_GROUPS) * D_MODEL:]], axis=1)
        planes = _project(x2d, _scale_q(w), seq=seq, dil=dil, tile=TILE)
        if dil == 1:
            gate_planes = planes
        group_planes.append(planes)
    y = _attn_b(group_planes, gate_planes, 3 * SECTION_PLANES, batch=batch, seq=seq)
    out = _out_ln(y.reshape(batch * seq, D_MODEL), p["w_out"].astype(BF16), x2d, p["ln_g"], p["ln_b"])
    return out.reshape(batch, seq, D_MODEL)


def _trunk(x, layers):
    for i, p in enumerate(layers):
        x = _layer_a(x, p, i) if i % 2 == 0 else _layer_b(x, p)
    return x


def kernel(x_prompt, x_sample, w_in_0, lam_q1_0, lam_k1_0, lam_q2_0, lam_k2_0, subln_g_0, w_out_0, ln_g_0, ln_b_0, w_in_1, w_out_1, ln_g_1, ln_b_1, w_in_2, lam_q1_2, lam_k1_2, lam_q2_2, lam_k2_2, subln_g_2, w_out_2, ln_g_2, ln_b_2, w_in_3, w_out_3, ln_g_3, ln_b_3):
    layers = [
        dict(w_in=w_in_0, lam_q1=lam_q1_0, lam_k1=lam_k1_0, lam_q2=lam_q2_0, lam_k2=lam_k2_0,
             subln_g=subln_g_0, w_out=w_out_0, ln_g=ln_g_0, ln_b=ln_b_0),
        dict(w_in=w_in_1, w_out=w_out_1, ln_g=ln_g_1, ln_b=ln_b_1),
        dict(w_in=w_in_2, lam_q1=lam_q1_2, lam_k1=lam_k1_2, lam_q2=lam_q2_2, lam_k2=lam_k2_2,
             subln_g=subln_g_2, w_out=w_out_2, ln_g=ln_g_2, ln_b=ln_b_2),
        dict(w_in=w_in_3, w_out=w_out_3, ln_g=ln_g_3, ln_b=ln_b_3),
    ]
    return (_trunk(x_prompt, layers), _trunk(x_sample, layers))
```

```python
import functools
import math

import jax
import jax.numpy as jnp
from jax import lax
from jax.experimental import pallas as pl
from jax.experimental.pallas import tpu as pltpu

F32 = jnp.float32
BF16 = jnp.bfloat16

D_MODEL = 1024
DEPTH = 4
ROPE_THETA = 500000.0
LN_EPS = 1e-5
ALPHA = (2.0 * DEPTH) ** 0.25
HEAD_DIM = 64
LANES = 128
N_PAIRS = D_MODEL // LANES
SECTION_PLANES = D_MODEL // LANES
B_GROUPS = ((128, 1), (512, 4), (2048, 16))
RADIUS = 64
TILE = 2048
HALO = 64
QBLK = 128
MASKED = -1e30
GROUP = 4
VMEM_LIMIT = 56 * 1024 * 1024


def _cparams(sem, flags=None):
    return pltpu.CompilerParams(dimension_semantics=sem, vmem_limit_bytes=VMEM_LIMIT, flags=flags)


def _rope_tables(seq, dil, tm):
    half = HEAD_DIM // 8
    pos = jnp.arange(seq, dtype=jnp.int32).reshape(seq // tm, tm // dil, dil).transpose(0, 2, 1).reshape(seq)
    dim = jnp.arange(LANES, dtype=jnp.int32) % HEAD_DIM
    rotated = dim < 2 * half
    inv = jnp.where(rotated, ROPE_THETA ** (-(dim % half).astype(F32) / half), 0.0)
    ang = pos.astype(F32)[:, None] * inv[None, :]
    sin = jnp.sin(ang)
    c = jnp.cos(ang)
    sa = sin * jnp.where(dim < half, -1.0, 0.0)[None, :]
    sb = sin * jnp.where(rotated & (dim >= half), 1.0, 0.0)[None, :]
    return c, sa, sb


X_SLABS = D_MODEL // LANES


ROPE_COLS = 2 * D_MODEL


def _proj_kernel(*refs, dil, tm, sub_n, n_out):
    x_refs = refs[:X_SLABS]
    w_ref, c_ref, sa_ref, sb_ref, o_ref, xs_ref = refs[X_SLABS:]
    res_rows = tm // dil

    for c in range(X_SLABS):
        for r in range(dil):
            src = x_refs[c][pl.ds(r, res_rows, stride=dil), :] if dil > 1 else x_refs[c][...]
            xs_ref[r * res_rows:(r + 1) * res_rows, c * LANES:(c + 1) * LANES] = src.astype(BF16)

    for sd in range(n_out // sub_n):
        acc = jnp.dot(xs_ref[...], w_ref[:, sd * sub_n:(sd + 1) * sub_n], preferred_element_type=F32)
        for t in range(sub_n // LANES):
            a = acc[:, t * LANES:(t + 1) * LANES]
            if sd * sub_n < ROPE_COLS:
                a = a * c_ref[...] + pltpu.roll(a, LANES - 8, 1) * sa_ref[...] + pltpu.roll(a, 8, 1) * sb_ref[...]
            ab = a.astype(BF16)
            for r in range(dil):
                o_ref[sd * (sub_n // LANES) + t, r] = ab[r * res_rows:(r + 1) * res_rows]


def _project(x2d, w, *, seq, dil, tile, tm=1024, sub_n=1024):
    n = x2d.shape[0]
    n_out = w.shape[1]
    per_tile = tile // tm
    n_seq_tiles = seq // tm
    tabs = _rope_tables(seq, dil, tm)
    kern = functools.partial(_proj_kernel, dil=dil, tm=tm, sub_n=sub_n, n_out=n_out)
    tab_spec = pl.BlockSpec((tm, LANES), lambda i: (i % n_seq_tiles, 0))
    x_specs = [pl.BlockSpec((tm, LANES), lambda i, c=c: (i, c)) for c in range(X_SLABS)]
    return pl.pallas_call(
        kern,
        grid=(n // tm,),
        in_specs=x_specs + [
            pl.BlockSpec((D_MODEL, n_out), lambda i: (0, 0)),
            tab_spec, tab_spec, tab_spec,
        ],
        out_specs=pl.BlockSpec((n_out // LANES, None, dil, tm // dil, LANES),
                               lambda i: (0, i // per_tile, 0, i % per_tile, 0)),
        out_shape=jax.ShapeDtypeStruct((n_out // LANES, n // tile, dil, tile // dil, LANES), BF16),
        scratch_shapes=[pltpu.VMEM((tm, D_MODEL), BF16)],
        compiler_params=_cparams(("arbitrary",)),
    )(*([x2d] * X_SLABS), w, *tabs)


N_SCORE_BUFS = 2


def _silu(g):
    return g * (1.0 / (1.0 + jnp.exp(-g)))


def _attn_a_kernel(lamv_ref, subg_ref, q_ref, k_ref, v_ref, g_ref, o_ref,
                   qt_ref, vt_ref, s0_ref, s1_ref, m_ref, l_ref, acc_ref, *, lam_init, bq, bk, nkv, nblk):
    @pl.when(pl.program_id(2) == 0)
    def _():
        for j in range(nkv):
            vt_ref[:, j * bk:(j + 1) * bk] = v_ref[j * bk:(j + 1) * bk, :].astype(F32).T.astype(BF16)

    for blk in range(nblk):
        qt = q_ref[blk * bq:(blk + 1) * bq, :].astype(F32).T
        dim = lax.broadcasted_iota(jnp.int32, qt.shape, 0)
        qt_ref[blk, :, 0:bq] = jnp.where(dim < HEAD_DIM, qt, 0.0).astype(BF16)
        qt_ref[blk, :, bq:2 * bq] = jnp.where(dim >= HEAD_DIM, qt, 0.0).astype(BF16)
    m_ref[...] = jnp.full(m_ref.shape, -jnp.inf, F32)
    l_ref[...] = jnp.zeros(l_ref.shape, F32)
    acc_ref[...] = jnp.zeros(acc_ref.shape, F32)

    def scores(blk, j, s_ref):
        kc = k_ref[j * bk:(j + 1) * bk, :]
        s_ref[...] = jnp.dot(kc, qt_ref[blk], preferred_element_type=F32)

    def update(blk, j, s_ref):
        m_prev = m_ref[blk]
        m_new = jnp.maximum(m_prev, jnp.max(s_ref[...], axis=0, keepdims=True))
        alpha = jnp.exp2(m_prev - m_new)
        p = jnp.exp2(s_ref[...] - m_new)
        l_ref[blk] = alpha * l_ref[blk] + jnp.sum(p, axis=0, keepdims=True)
        pv = jnp.dot(vt_ref[:, j * bk:(j + 1) * bk], p.astype(BF16), preferred_element_type=F32)
        acc_ref[blk] = alpha * acc_ref[blk] + pv
        m_ref[blk] = m_new

    def finalize(blk):
        lv = lamv_ref[...]
        lam = (jnp.exp(jnp.sum(lv[0:1] * lv[1:2], axis=1, keepdims=True))
               - jnp.exp(jnp.sum(lv[2:3] * lv[3:4], axis=1, keepdims=True)) + lam_init)
        ot = acc_ref[blk] / l_ref[blk]
        odt = ot[:, 0:bq] - lam * ot[:, bq:2 * bq]
        odt = odt * lax.rsqrt(jnp.mean(odt * odt, axis=0, keepdims=True) + LN_EPS)
        od = odt.T * subg_ref[...] * (1.0 - lam_init)
        rows = slice(blk * bq, (blk + 1) * bq)
        o_ref[rows, :] = (od * _silu(g_ref[rows, :].astype(F32))).astype(BF16)

    work = [(blk, j) for blk in range(nblk) for j in range(nkv)]
    bufs = (s0_ref, s1_ref)
    scores(*work[0], bufs[0])
    for i, (blk, j) in enumerate(work):
        if i + 1 < len(work):
            scores(*work[i + 1], bufs[(i + 1) % N_SCORE_BUFS])
        update(blk, j, bufs[i % N_SCORE_BUFS])
        if j == nkv - 1:
            finalize(blk)


def _attn_a(planes, lamv, subg, *, batch, seq, lam_init, bq=512, bk=1024, nblk=4):
    pl4 = planes.reshape(4 * SECTION_PLANES, batch, seq, LANES)
    nkv = seq // bk
    rows = nblk * bq
    assert seq % bk == 0 and seq % rows == 0
    kern = functools.partial(_attn_a_kernel, lam_init=lam_init, bq=bq, bk=bk, nkv=nkv, nblk=nblk)

    def q_rows(s):
        return pl.BlockSpec((None, None, rows, LANES), lambda b, h, qi: (s * SECTION_PLANES + h, b, qi, 0))

    def all_rows(s):
        return pl.BlockSpec((None, None, seq, LANES), lambda b, h, qi: (s * SECTION_PLANES + h, b, 0, 0))

    return pl.pallas_call(
        kern,
        grid=(batch, N_PAIRS, seq // rows),
        in_specs=[
            pl.BlockSpec((4, HEAD_DIM), lambda b, h, qi: (0, 0)),
            pl.BlockSpec((1, LANES), lambda b, h, qi: (0, 0)),
            q_rows(0), all_rows(1), all_rows(2), q_rows(3),
        ],
        out_specs=pl.BlockSpec((None, rows, LANES), lambda b, h, qi: (b, qi, h)),
        out_shape=jax.ShapeDtypeStruct((batch, seq, D_MODEL), BF16),
        scratch_shapes=[
            pltpu.VMEM((nblk, LANES, 2 * bq), BF16),
            pltpu.VMEM((LANES, seq), BF16),
        ] + [pltpu.VMEM((bk, 2 * bq), F32)] * N_SCORE_BUFS + [
            pltpu.VMEM((nblk, 1, 2 * bq), F32),
            pltpu.VMEM((nblk, 1, 2 * bq), F32),
            pltpu.VMEM((nblk, LANES, 2 * bq), F32),
        ],
        compiler_params=_cparams(("arbitrary", "arbitrary", "arbitrary")),
    )(lamv, subg, pl4, pl4, pl4, pl4)


def _attn_b_group(gi, dil, seq, q_ref, k_ref, kp_ref, kn_ref, v_ref, vp_ref, vn_ref,
                  kw_ref, vw_ref, og_ref, mg_ref, lg_ref, bias_ref):
    rows = TILE // dil
    nb64 = rows // HALO
    tile_i = pl.program_id(2)
    length = seq // dil

    kw_ref[0:dil, 0:HALO, :] = kp_ref[...]
    vw_ref[0:dil, 0:HALO, :] = vp_ref[...]
    for n in range(nb64):
        kw_ref[0:dil, HALO + n * HALO:2 * HALO + n * HALO, :] = k_ref[:, n]
        vw_ref[0:dil, HALO + n * HALO:2 * HALO + n * HALO, :] = v_ref[:, n]
    kw_ref[0:dil, HALO + rows:2 * HALO + rows, :] = kn_ref[...]
    vw_ref[0:dil, HALO + rows:2 * HALO + rows, :] = vn_ref[...]

    win = QBLK + 2 * HALO
    lane_q = lax.broadcasted_iota(jnp.int32, (QBLK, LANES), 1)
    lo_q = lane_q < HEAD_DIM
    blocks = rows // QBLK
    unroll = max(1, 8 // blocks)

    def block_scores(r, blk):
        q2 = q_ref[r, 2 * blk:2 * blk + 2].reshape(QBLK, LANES)
        zero = jnp.zeros_like(q2)
        qs = jnp.concatenate([jnp.where(lo_q, q2, zero), jnp.where(lo_q, zero, q2)], axis=0)
        kwin = kw_ref[r, blk * QBLK:blk * QBLK + win, :]
        s = lax.dot_general(qs, kwin, (((1,), (1,)), ((), ())), preferred_element_type=F32)
        first = tile_i * rows + blk * QBLK
        edge = (first == 0).astype(jnp.int32) + 2 * (first + QBLK == length).astype(jnp.int32)
        return s + bias_ref[edge]

    def block_softmax(s):
        m = jnp.max(s, axis=1, keepdims=True)
        p = jnp.exp2(s - m)
        return p.astype(BF16), m, jnp.sum(p, axis=1, keepdims=True)

    def per_head(a):
        return jnp.where(lo_q, jnp.broadcast_to(a[0:QBLK], (QBLK, LANES)),
                         jnp.broadcast_to(a[QBLK:2 * QBLK], (QBLK, LANES)))

    def block_output(r, blk, pb, m, l):
        vwin = vw_ref[r, blk * QBLK:blk * QBLK + win, :]
        o0 = jnp.dot(pb[0:QBLK], vwin, preferred_element_type=F32)
        o1 = jnp.dot(pb[QBLK:2 * QBLK], vwin, preferred_element_type=F32)
        tok = pl.ds(r + blk * QBLK * dil, QBLK, stride=dil) if dil > 1 else pl.ds(blk * QBLK, QBLK)
        og_ref[gi, tok, :] = jnp.where(lo_q, o0, o1)
        mg_ref[gi, tok, :] = per_head(m)
        lg_ref[gi, tok, :] = per_head(l)

    def body(it, carry):
        items = [(it * unroll + u, blk) for u in range(unroll) for blk in range(blocks)]
        for g0 in range(0, len(items), GROUP):
            group = items[g0:g0 + GROUP]
            scores = [block_scores(r, blk) for r, blk in group]
            soft = [block_softmax(s) for s in scores]
            for (r, blk), (pb, m, l) in zip(group, soft):
                block_output(r, blk, pb, m, l)
        return carry

    if dil == 1:
        body(0, 0)
    else:
        lax.fori_loop(0, dil // unroll, body, 0)


def _attn_b_kernel(*refs, seq):
    n_in = 7 * len(B_GROUPS) + 1
    gate_ref = refs[n_in - 1]
    o_ref = refs[n_in]
    kw_ref, vw_ref, og_ref, mg_ref, lg_ref, bias_ref = refs[n_in + 1:]

    @pl.when((pl.program_id(0) == 0) & (pl.program_id(1) == 0) & (pl.program_id(2) == 0))
    def _():
        win = QBLK + 2 * HALO
        row_i = lax.broadcasted_iota(jnp.int32, (2 * QBLK, win), 0)
        col_i = lax.broadcasted_iota(jnp.int32, (2 * QBLK, win), 1)
        qpos = jnp.where(row_i >= QBLK, row_i - QBLK, row_i)
        band = jnp.where(jnp.abs(col_i - HALO - qpos) <= RADIUS, 0.0, MASKED)
        low = jnp.where(col_i < HALO, MASKED, 0.0)
        high = jnp.where(col_i >= QBLK + HALO, MASKED, 0.0)
        bias_ref[0] = band
        bias_ref[1] = band + low
        bias_ref[2] = band + high
        bias_ref[3] = band + low + high

    for gi, (_, dil) in enumerate(B_GROUPS):
        _attn_b_group(gi, dil, seq, *refs[7 * gi:7 * gi + 7], kw_ref, vw_ref, og_ref, mg_ref, lg_ref, bias_ref)
    m0, m1, m2 = mg_ref[0], mg_ref[1], mg_ref[2]
    mx = jnp.maximum(jnp.maximum(m0, m1), m2)
    e0, e1, e2 = jnp.exp2(m0 - mx), jnp.exp2(m1 - mx), jnp.exp2(m2 - mx)
    den = e0 * lg_ref[0] + e1 * lg_ref[1] + e2 * lg_ref[2]
    o = (e0 * og_ref[0] + e1 * og_ref[1] + e2 * og_ref[2]) / den
    o_ref[...] = (o * _silu(gate_ref[...].astype(F32))).astype(BF16)


def _attn_b(group_planes, gate_planes, gate_base, *, batch, seq):
    n_tiles = seq // TILE
    in_specs, args = [], []
    for (_, dil), planes in zip(B_GROUPS, group_planes):
        rows = TILE // dil
        nb64 = rows // HALO
        view = planes.reshape(planes.shape[0], batch, n_tiles, dil, nb64, HALO, LANES)

        def main(s):
            return pl.BlockSpec((None, None, None, dil, nb64, HALO, LANES),
                                lambda b, h, i, s=s: (s * SECTION_PLANES + h, b, i, 0, 0, 0, 0))

        def prev(s, nb64=nb64):
            return pl.BlockSpec((None, None, None, dil, None, HALO, LANES),
                                lambda b, h, i, s=s: (s * SECTION_PLANES + h, b, jnp.maximum(i - 1, 0), 0, nb64 - 1, 0, 0))

        def nxt(s):
            return pl.BlockSpec((None, None, None, dil, None, HALO, LANES),
                                lambda b, h, i, s=s: (s * SECTION_PLANES + h, b, jnp.minimum(i + 1, n_tiles - 1), 0, 0, 0, 0))

        in_specs += [main(0), main(1), prev(1), nxt(1), main(2), prev(2), nxt(2)]
        args += [view] * 7
    gate_view = gate_planes.reshape(gate_planes.shape[0], batch, seq, LANES)
    in_specs.append(pl.BlockSpec((None, None, TILE, LANES), lambda b, h, i: (gate_base + h, b, i, 0)))
    args.append(gate_view)
    max_dil = max(d for _, d in B_GROUPS)
    return pl.pallas_call(
        functools.partial(_attn_b_kernel, seq=seq),
        grid=(batch, N_PAIRS, n_tiles),
        in_specs=in_specs,
        out_specs=pl.BlockSpec((None, TILE, LANES), lambda b, h, i: (b, i, h)),
        out_shape=jax.ShapeDtypeStruct((batch, seq, D_MODEL), BF16),
        scratch_shapes=[
            pltpu.VMEM((max_dil, TILE + 2 * HALO, LANES), BF16),
            pltpu.VMEM((max_dil, TILE + 2 * HALO, LANES), BF16),
            pltpu.VMEM((len(B_GROUPS), TILE, LANES), F32),
            pltpu.VMEM((len(B_GROUPS), TILE, LANES), F32),
            pltpu.VMEM((len(B_GROUPS), TILE, LANES), F32),
            pltpu.VMEM((4, 2 * QBLK, QBLK + 2 * HALO), F32),
        ],
        compiler_params=_cparams(("arbitrary", "arbitrary", "arbitrary")),
    )(*args)


def _out_ln_kernel(y_ref, w_ref, x_ref, g_ref, b_ref, o_ref, *, tm, rc):
    for r0 in range(0, tm, rc):
        rows = slice(r0, r0 + rc)
        f = jnp.dot(y_ref[rows, :], w_ref[...], preferred_element_type=F32)
        z = ALPHA * x_ref[rows, :] + f
        mu = jnp.mean(z, axis=1, keepdims=True)
        zc = z - mu
        var = jnp.mean(zc * zc, axis=1, keepdims=True)
        o_ref[rows, :] = zc * lax.rsqrt(var + LN_EPS) * g_ref[...] + b_ref[...]


def _out_ln(y2d, w, x2d, g, b, *, tm=1024, rc=512):
    n = x2d.shape[0]
    return pl.pallas_call(
        functools.partial(_out_ln_kernel, tm=tm, rc=rc),
        grid=(n // tm,),
        in_specs=[
            pl.BlockSpec((tm, D_MODEL), lambda i: (i, 0)),
            pl.BlockSpec((D_MODEL, D_MODEL), lambda i: (0, 0)),
            pl.BlockSpec((tm, D_MODEL), lambda i: (i, 0)),
            pl.BlockSpec((1, D_MODEL), lambda i: (0, 0)),
            pl.BlockSpec((1, D_MODEL), lambda i: (0, 0)),
        ],
        out_specs=pl.BlockSpec((tm, D_MODEL), lambda i: (i, 0)),
        out_shape=jax.ShapeDtypeStruct((n, D_MODEL), F32),
        compiler_params=_cparams(("arbitrary",)),
    )(y2d, w, x2d, g.reshape(1, D_MODEL), b.reshape(1, D_MODEL))


Q_SCALE_LOG2 = math.log2(math.e) / math.sqrt(HEAD_DIM)


def _scale_q(w):
    return jnp.concatenate([w[:, :D_MODEL] * Q_SCALE_LOG2, w[:, D_MODEL:]], axis=1).astype(BF16)


def _layer_a(x, p, layer_idx):
    batch, seq, _ = x.shape
    x2d = x.reshape(batch * seq, D_MODEL)
    w_in = _scale_q(p["w_in"])
    planes = _project(x2d, w_in, seq=seq, dil=1, tile=1024)
    lam_init = 0.8 - 0.6 * math.exp(-0.3 * layer_idx)
    lamv = jnp.stack([p["lam_q1"], p["lam_k1"], p["lam_q2"], p["lam_k2"]]).astype(F32)
    y = _attn_a(planes, lamv, p["subln_g"].astype(F32).reshape(1, LANES), batch=batch, seq=seq, lam_init=lam_init)
    out = _out_ln(y.reshape(batch * seq, D_MODEL), p["w_out"].astype(BF16), x2d, p["ln_g"], p["ln_b"])
    return out.reshape(batch, seq, D_MODEL)


def _layer_b(x, p):
    batch, seq, _ = x.shape
    x2d = x.reshape(batch * seq, D_MODEL)
    w_in = p["w_in"]
    group_planes = []
    gate_planes = None
    for gi, (_, dil) in enumerate(B_GROUPS):
        w = w_in[:, 3 * gi * D_MODEL:3 * (gi + 1) * D_MODEL]
        if dil == 1:
            w = jnp.concatenate([w, w_in[:, 3 * len(B_GROUPS) * D_MODEL:]], axis=1)
        planes = _project(x2d, _scale_q(w), seq=seq, dil=dil, tile=TILE)
        if dil == 1:
            gate_planes = planes
        group_planes.append(planes)
    y = _attn_b(group_planes, gate_planes, 3 * SECTION_PLANES, batch=batch, seq=seq)
    out = _out_ln(y.reshape(batch * seq, D_MODEL), p["w_out"].astype(BF16), x2d, p["ln_g"], p["ln_b"])
    return out.reshape(batch, seq, D_MODEL)


def _trunk(x, layers):
    for i, p in enumerate(layers):
        x = _layer_a(x, p, i) if i % 2 == 0 else _layer_b(x, p)
    return x


def kernel(x_prompt, x_sample, w_in_0, lam_q1_0, lam_k1_0, lam_q2_0, lam_k2_0, subln_g_0, w_out_0, ln_g_0, ln_b_0, w_in_1, w_out_1, ln_g_1, ln_b_1, w_in_2, lam_q1_2, lam_k1_2, lam_q2_2, lam_k2_2, subln_g_2, w_out_2, ln_g_2, ln_b_2, w_in_3, w_out_3, ln_g_3, ln_b_3):
    layers = [
        dict(w_in=w_in_0, lam_q1=lam_q1_0, lam_k1=lam_k1_0, lam_q2=lam_q2_0, lam_k2=lam_k2_0,
             subln_g=subln_g_0, w_out=w_out_0, ln_g=ln_g_0, ln_b=ln_b_0),
        dict(w_in=w_in_1, w_out=w_out_1, ln_g=ln_g_1, ln_b=ln_b_1),
        dict(w_in=w_in_2, lam_q1=lam_q1_2, lam_k1=lam_k1_2, lam_q2=lam_q2_2, lam_k2=lam_k2_2,
             subln_g=subln_g_2, w_out=w_out_2, ln_g=ln_g_2, ln_b=ln_b_2),
        dict(w_in=w_in_3, w_out=w_out_3, ln_g=ln_g_3, ln_b=ln_b_3),
    ]
    return (_trunk(x_prompt, layers), _trunk(x_sample, layers))
```

```python
import functools
import math

import jax
import jax.numpy as jnp
from jax import lax
from jax.experimental import pallas as pl
from jax.experimental.pallas import tpu as pltpu

F32 = jnp.float32
BF16 = jnp.bfloat16

D_MODEL = 1024
DEPTH = 4
ROPE_THETA = 500000.0
LN_EPS = 1e-5
ALPHA = (2.0 * DEPTH) ** 0.25
HEAD_DIM = 64
LANES = 128
N_PAIRS = D_MODEL // LANES
SECTION_PLANES = D_MODEL // LANES
B_GROUPS = ((128, 1), (512, 4), (2048, 16))
RADIUS = 64
TILE = 2048
HALO = 64
QBLK = 128
MASKED = -1e30
GROUP = 4
VMEM_LIMIT = 56 * 1024 * 1024


def _cparams(sem, flags=None):
    return pltpu.CompilerParams(dimension_semantics=sem, vmem_limit_bytes=VMEM_LIMIT, flags=flags)


def _rope_tables(seq, dil, tm):
    half = HEAD_DIM // 8
    pos = jnp.arange(seq, dtype=jnp.int32).reshape(seq // tm, tm // dil, dil).transpose(0, 2, 1).reshape(seq)
    dim = jnp.arange(LANES, dtype=jnp.int32) % HEAD_DIM
    rotated = dim < 2 * half
    inv = jnp.where(rotated, ROPE_THETA ** (-(dim % half).astype(F32) / half), 0.0)
    ang = pos.astype(F32)[:, None] * inv[None, :]
    sin = jnp.sin(ang)
    c = jnp.cos(ang)
    sa = sin * jnp.where(dim < half, -1.0, 0.0)[None, :]
    sb = sin * jnp.where(rotated & (dim >= half), 1.0, 0.0)[None, :]
    return c, sa, sb


X_SLABS = D_MODEL // LANES


ROPE_COLS = 2 * D_MODEL


def _proj_kernel(*refs, dil, tm, sub_n, n_out):
    x_refs = refs[:X_SLABS]
    w_ref, c_ref, sa_ref, sb_ref, o_ref, xs_ref = refs[X_SLABS:]
    res_rows = tm // dil

    for c in range(X_SLABS):
        for r in range(dil):
            src = x_refs[c][pl.ds(r, res_rows, stride=dil), :] if dil > 1 else x_refs[c][...]
            xs_ref[r * res_rows:(r + 1) * res_rows, c * LANES:(c + 1) * LANES] = src.astype(BF16)

    for sd in range(n_out // sub_n):
        acc = jnp.dot(xs_ref[...], w_ref[:, sd * sub_n:(sd + 1) * sub_n], preferred_element_type=F32)
        for t in range(sub_n // LANES):
            a = acc[:, t * LANES:(t + 1) * LANES]
            if sd * sub_n < ROPE_COLS:
                a = a * c_ref[...] + pltpu.roll(a, LANES - 8, 1) * sa_ref[...] + pltpu.roll(a, 8, 1) * sb_ref[...]
            ab = a.astype(BF16)
            for r in range(dil):
                o_ref[sd * (sub_n // LANES) + t, r] = ab[r * res_rows:(r + 1) * res_rows]


def _project(x2d, w, *, seq, dil, tile, tm=1024, sub_n=1024):
    n = x2d.shape[0]
    n_out = w.shape[1]
    per_tile = tile // tm
    n_seq_tiles = seq // tm
    tabs = _rope_tables(seq, dil, tm)
    kern = functools.partial(_proj_kernel, dil=dil, tm=tm, sub_n=sub_n, n_out=n_out)
    tab_spec = pl.BlockSpec((tm, LANES), lambda i: (i % n_seq_tiles, 0))
    x_specs = [pl.BlockSpec((tm, LANES), lambda i, c=c: (i, c)) for c in range(X_SLABS)]
    return pl.pallas_call(
        kern,
        grid=(n // tm,),
        in_specs=x_specs + [
            pl.BlockSpec((D_MODEL, n_out), lambda i: (0, 0)),
            tab_spec, tab_spec, tab_spec,
        ],
        out_specs=pl.BlockSpec((n_out // LANES, None, dil, tm // dil, LANES),
                               lambda i: (0, i // per_tile, 0, i % per_tile, 0)),
        out_shape=jax.ShapeDtypeStruct((n_out // LANES, n // tile, dil, tile // dil, LANES), BF16),
        scratch_shapes=[pltpu.VMEM((tm, D_MODEL), BF16)],
        compiler_params=_cparams(("arbitrary",)),
    )(*([x2d] * X_SLABS), w, *tabs)


N_SCORE_BUFS = 2


def _silu(g):
    return g * (1.0 / (1.0 + jnp.exp(-g)))


def _attn_a_kernel(lamv_ref, subg_ref, q_ref, k_ref, v_ref, g_ref, o_ref,
                   qt_ref, vt_ref, s0_ref, s1_ref, m_ref, l_ref, acc_ref, *, lam_init, bq, bk, nkv, nblk):
    @pl.when(pl.program_id(2) == 0)
    def _():
        for j in range(nkv):
            vt_ref[:, j * bk:(j + 1) * bk] = v_ref[j * bk:(j + 1) * bk, :].astype(F32).T.astype(BF16)

    for blk in range(nblk):
        qt = q_ref[blk * bq:(blk + 1) * bq, :].astype(F32).T
        dim = lax.broadcasted_iota(jnp.int32, qt.shape, 0)
        qt_ref[blk, :, 0:bq] = jnp.where(dim < HEAD_DIM, qt, 0.0).astype(BF16)
        qt_ref[blk, :, bq:2 * bq] = jnp.where(dim >= HEAD_DIM, qt, 0.0).astype(BF16)
    m_ref[...] = jnp.full(m_ref.shape, -jnp.inf, F32)
    l_ref[...] = jnp.zeros(l_ref.shape, F32)
    acc_ref[...] = jnp.zeros(acc_ref.shape, F32)

    def scores(blk, j, s_ref):
        kc = k_ref[j * bk:(j + 1) * bk, :]
        s_ref[...] = jnp.dot(kc, qt_ref[blk], preferred_element_type=F32)

    def update(blk, j, s_ref):
        m_prev = m_ref[blk]
        m_new = jnp.maximum(m_prev, jnp.max(s_ref[...], axis=0, keepdims=True))
        alpha = jnp.exp2(m_prev - m_new)
        p = jnp.exp2(s_ref[...] - m_new)
        l_ref[blk] = alpha * l_ref[blk] + jnp.sum(p, axis=0, keepdims=True)
        pv = jnp.dot(vt_ref[:, j * bk:(j + 1) * bk], p.astype(BF16), preferred_element_type=F32)
        acc_ref[blk] = alpha * acc_ref[blk] + pv
        m_ref[blk] = m_new

    def finalize(blk):
        lv = lamv_ref[...]
        lam = (jnp.exp(jnp.sum(lv[0:1] * lv[1:2], axis=1, keepdims=True))
               - jnp.exp(jnp.sum(lv[2:3] * lv[3:4], axis=1, keepdims=True)) + lam_init)
        ot = acc_ref[blk] / l_ref[blk]
        odt = ot[:, 0:bq] - lam * ot[:, bq:2 * bq]
        odt = odt * lax.rsqrt(jnp.mean(odt * odt, axis=0, keepdims=True) + LN_EPS)
        od = odt.T * subg_ref[...] * (1.0 - lam_init)
        rows = slice(blk * bq, (blk + 1) * bq)
        o_ref[rows, :] = (od * _silu(g_ref[rows, :].astype(F32))).astype(BF16)

    work = [(blk, j) for blk in range(nblk) for j in range(nkv)]
    bufs = (s0_ref, s1_ref)
    scores(*work[0], bufs[0])
    for i, (blk, j) in enumerate(work):
        if i + 1 < len(work):
            scores(*work[i + 1], bufs[(i + 1) % N_SCORE_BUFS])
        update(blk, j, bufs[i % N_SCORE_BUFS])
        if j == nkv - 1:
            finalize(blk)


def _attn_a(planes, lamv, subg, *, batch, seq, lam_init, bq=512, bk=1024, nblk=4):
    pl4 = planes.reshape(4 * SECTION_PLANES, batch, seq, LANES)
    nkv = seq // bk
    rows = nblk * bq
    assert seq % bk == 0 and seq % rows == 0
    kern = functools.partial(_attn_a_kernel, lam_init=lam_init, bq=bq, bk=bk, nkv=nkv, nblk=nblk)

    def q_rows(s):
        return pl.BlockSpec((None, None, rows, LANES), lambda b, h, qi: (s * SECTION_PLANES + h, b, qi, 0))

    def all_rows(s):
        return pl.BlockSpec((None, None, seq, LANES), lambda b, h, qi: (s * SECTION_PLANES + h, b, 0, 0))

    return pl.pallas_call(
        kern,
        grid=(batch, N_PAIRS, seq // rows),
        in_specs=[
            pl.BlockSpec((4, HEAD_DIM), lambda b, h, qi: (0, 0)),
            pl.BlockSpec((1, LANES), lambda b, h, qi: (0, 0)),
            q_rows(0), all_rows(1), all_rows(2), q_rows(3),
        ],
        out_specs=pl.BlockSpec((None, rows, LANES), lambda b, h, qi: (b, qi, h)),
        out_shape=jax.ShapeDtypeStruct((batch, seq, D_MODEL), BF16),
        scratch_shapes=[
            pltpu.VMEM((nblk, LANES, 2 * bq), BF16),
            pltpu.VMEM((LANES, seq), BF16),
        ] + [pltpu.VMEM((bk, 2 * bq), F32)] * N_SCORE_BUFS + [
            pltpu.VMEM((nblk, 1, 2 * bq), F32),
            pltpu.VMEM((nblk, 1, 2 * bq), F32),
            pltpu.VMEM((nblk, LANES, 2 * bq), F32),
        ],
        compiler_params=_cparams(("arbitrary", "arbitrary", "arbitrary")),
    )(lamv, subg, pl4, pl4, pl4, pl4)


def _attn_b_group(gi, dil, seq, q_ref, k_ref, kp_ref, kn_ref, v_ref, vp_ref, vn_ref,
                  kw_ref, vw_ref, og_ref, mg_ref, lg_ref, bias_ref):
    rows = TILE // dil
    nb64 = rows // HALO
    tile_i = pl.program_id(2)
    length = seq // dil

    kw_ref[0:dil, 0:HALO, :] = kp_ref[...]
    vw_ref[0:dil, 0:HALO, :] = vp_ref[...]
    for n in range(nb64):
        kw_ref[0:dil, HALO + n * HALO:2 * HALO + n * HALO, :] = k_ref[:, n]
        vw_ref[0:dil, HALO + n * HALO:2 * HALO + n * HALO, :] = v_ref[:, n]
    kw_ref[0:dil, HALO + rows:2 * HALO + rows, :] = kn_ref[...]
    vw_ref[0:dil, HALO + rows:2 * HALO + rows, :] = vn_ref[...]

    win = QBLK + 2 * HALO
    lane_q = lax.broadcasted_iota(jnp.int32, (QBLK, LANES), 1)
    lo_q = lane_q < HEAD_DIM
    blocks = rows // QBLK
    unroll = max(1, 8 // blocks)

    def block_scores(r, blk):
        q2 = q_ref[r, 2 * blk:2 * blk + 2].reshape(QBLK, LANES)
        zero = jnp.zeros_like(q2)
        qs = jnp.concatenate([jnp.where(lo_q, q2, zero), jnp.where(lo_q, zero, q2)], axis=0)
        kwin = kw_ref[r, blk * QBLK:blk * QBLK + win, :]
        s = lax.dot_general(qs, kwin, (((1,), (1,)), ((), ())), preferred_element_type=F32)
        first = tile_i * rows + blk * QBLK
        edge = (first == 0).astype(jnp.int32) + 2 * (first + QBLK == length).astype(jnp.int32)
        return s + bias_ref[edge]

    def block_softmax(s):
        m = jnp.max(s, axis=1, keepdims=True)
        p = jnp.exp2(s - m)
        return p.astype(BF16), m, jnp.sum(p, axis=1, keepdims=True)

    def per_head(a):
        return jnp.where(lo_q, jnp.broadcast_to(a[0:QBLK], (QBLK, LANES)),
                         jnp.broadcast_to(a[QBLK:2 * QBLK], (QBLK, LANES)))

    def block_output(r, blk, pb, m, l):
        vwin = vw_ref[r, blk * QBLK:blk * QBLK + win, :]
        o0 = jnp.dot(pb[0:QBLK], vwin, preferred_element_type=F32)
        o1 = jnp.dot(pb[QBLK:2 * QBLK], vwin, preferred_element_type=F32)
        tok = pl.ds(r + blk * QBLK * dil, QBLK, stride=dil) if dil > 1 else pl.ds(blk * QBLK, QBLK)
        og_ref[gi, tok, :] = jnp.where(lo_q, o0, o1)
        mg_ref[gi, tok, :] = per_head(m)
        lg_ref[gi, tok, :] = per_head(l)

    def body(it, carry):
        items = [(it * unroll + u, blk) for u in range(unroll) for blk in range(blocks)]
        for g0 in range(0, len(items), GROUP):
            group = items[g0:g0 + GROUP]
            scores = [block_scores(r, blk) for r, blk in group]
            soft = [block_softmax(s) for s in scores]
            for (r, blk), (pb, m, l) in zip(group, soft):
                block_output(r, blk, pb, m, l)
        return carry

    if dil == 1:
        body(0, 0)
    else:
        lax.fori_loop(0, dil // unroll, body, 0)


def _attn_b_kernel(*refs, seq):
    n_in = 7 * len(B_GROUPS) + 1
    gate_ref = refs[n_in - 1]
    o_ref = refs[n_in]
    kw_ref, vw_ref, og_ref, mg_ref, lg_ref, bias_ref = refs[n_in + 1:]

    @pl.when((pl.program_id(0) == 0) & (pl.program_id(1) == 0) & (pl.program_id(2) == 0))
    def _():
        win = QBLK + 2 * HALO
        row_i = lax.broadcasted_iota(jnp.int32, (2 * QBLK, win), 0)
        col_i = lax.broadcasted_iota(jnp.int32, (2 * QBLK, win), 1)
        qpos = jnp.where(row_i >= QBLK, row_i - QBLK, row_i)
        band = jnp.where(jnp.abs(col_i - HALO - qpos) <= RADIUS, 0.0, MASKED)
        low = jnp.where(col_i < HALO, MASKED, 0.0)
        high = jnp.where(col_i >= QBLK + HALO, MASKED, 0.0)
        bias_ref[0] = band
        bias_ref[1] = band + low
        bias_ref[2] = band + high
        bias_ref[3] = band + low + high

    for gi, (_, dil) in enumerate(B_GROUPS):
        _attn_b_group(gi, dil, seq, *refs[7 * gi:7 * gi + 7], kw_ref, vw_ref, og_ref, mg_ref, lg_ref, bias_ref)
    m0, m1, m2 = mg_ref[0], mg_ref[1], mg_ref[2]
    mx = jnp.maximum(jnp.maximum(m0, m1), m2)
    e0, e1, e2 = jnp.exp2(m0 - mx), jnp.exp2(m1 - mx), jnp.exp2(m2 - mx)
    den = e0 * lg_ref[0] + e1 * lg_ref[1] + e2 * lg_ref[2]
    o = (e0 * og_ref[0] + e1 * og_ref[1] + e2 * og_ref[2]) / den
    o_ref[...] = (o * _silu(gate_ref[...].astype(F32))).astype(BF16)


def _attn_b(group_planes, gate_planes, gate_base, *, batch, seq):
    n_tiles = seq // TILE
    in_specs, args = [], []
    for (_, dil), planes in zip(B_GROUPS, group_planes):
        rows = TILE // dil
        nb64 = rows // HALO
        view = planes.reshape(planes.shape[0], batch, n_tiles, dil, nb64, HALO, LANES)

        def main(s):
            return pl.BlockSpec((None, None, None, dil, nb64, HALO, LANES),
                                lambda b, h, i, s=s: (s * SECTION_PLANES + h, b, i, 0, 0, 0, 0))

        def prev(s, nb64=nb64):
            return pl.BlockSpec((None, None, None, dil, None, HALO, LANES),
                                lambda b, h, i, s=s: (s * SECTION_PLANES + h, b, jnp.maximum(i - 1, 0), 0, nb64 - 1, 0, 0))

        def nxt(s):
            return pl.BlockSpec((None, None, None, dil, None, HALO, LANES),
                                lambda b, h, i, s=s: (s * SECTION_PLANES + h, b, jnp.minimum(i + 1, n_tiles - 1), 0, 0, 0, 0))

        in_specs += [main(0), main(1), prev(1), nxt(1), main(2), prev(2), nxt(2)]
        args += [view] * 7
    gate_view = gate_planes.reshape(gate_planes.shape[0], batch, seq, LANES)
    in_specs.append(pl.BlockSpec((None, None, TILE, LANES), lambda b, h, i: (gate_base + h, b, i, 0)))
    args.append(gate_view)
    max_dil = max(d for _, d in B_GROUPS)
    return pl.pallas_call(
        functools.partial(_attn_b_kernel, seq=seq),
        grid=(batch, N_PAIRS, n_tiles),
        in_specs=in_specs,
        out_specs=pl.BlockSpec((None, TILE, LANES), lambda b, h, i: (b, i, h)),
        out_shape=jax.ShapeDtypeStruct((batch, seq, D_MODEL), BF16),
        scratch_shapes=[
            pltpu.VMEM((max_dil, TILE + 2 * HALO, LANES), BF16),
            pltpu.VMEM((max_dil, TILE + 2 * HALO, LANES), BF16),
            pltpu.VMEM((len(B_GROUPS), TILE, LANES), F32),
            pltpu.VMEM((len(B_GROUPS), TILE, LANES), F32),
            pltpu.VMEM((len(B_GROUPS), TILE, LANES), F32),
            pltpu.VMEM((4, 2 * QBLK, QBLK + 2 * HALO), F32),
        ],
        compiler_params=_cparams(("arbitrary", "arbitrary", "arbitrary")),
    )(*args)


def _out_ln_kernel(y_ref, w_ref, x_ref, g_ref, b_ref, o_ref, *, tm, rc):
    for r0 in range(0, tm, rc):
        rows = slice(r0, r0 + rc)
        f = jnp.dot(y_ref[rows, :], w_ref[...], preferred_element_type=F32)
        z = ALPHA * x_ref[rows, :] + f
        mu = jnp.mean(z, axis=1, keepdims=True)
        zc = z - mu
        var = jnp.mean(zc * zc, axis=1, keepdims=True)
        o_ref[rows, :] = zc * lax.rsqrt(var + LN_EPS) * g_ref[...] + b_ref[...]


def _out_ln(y2d, w, x2d, g, b, *, tm=1024, rc=512):
    n = x2d.shape[0]
    body = functools.partial(_out_ln_kernel, tm=tm, rc=rc)
    rows3 = pl.BlockSpec((tm, D_MODEL), lambda i: (i, 0), pipeline_mode=pl.Buffered(3))
    rows2 = pl.BlockSpec((tm, D_MODEL), lambda i: (i, 0))

    def outer(y_hbm, w_ref, x_hbm, g_ref, b_ref, o_hbm):
        def step(y_ref, x_ref, o_ref):
            body(y_ref, w_ref, x_ref, g_ref, b_ref, o_ref)

        pltpu.emit_pipeline(step, grid=(n // tm,), in_specs=[rows3, rows3], out_specs=[rows2])(y_hbm, x_hbm, o_hbm)

    any_space = pl.BlockSpec(memory_space=pl.ANY)
    vmem = pl.BlockSpec(memory_space=pltpu.VMEM)
    return pl.pallas_call(
        outer,
        in_specs=[any_space, vmem, any_space, vmem, vmem],
        out_specs=any_space,
        out_shape=jax.ShapeDtypeStruct((n, D_MODEL), F32),
        compiler_params=pltpu.CompilerParams(vmem_limit_bytes=VMEM_LIMIT),
    )(y2d, w, x2d, g.reshape(1, D_MODEL), b.reshape(1, D_MODEL))


Q_SCALE_LOG2 = math.log2(math.e) / math.sqrt(HEAD_DIM)


def _scale_q(w):
    return jnp.concatenate([w[:, :D_MODEL] * Q_SCALE_LOG2, w[:, D_MODEL:]], axis=1).astype(BF16)


def _layer_a(x, p, layer_idx):
    batch, seq, _ = x.shape
    x2d = x.reshape(batch * seq, D_MODEL)
    w_in = _scale_q(p["w_in"])
    planes = _project(x2d, w_in, seq=seq, dil=1, tile=1024)
    lam_init = 0.8 - 0.6 * math.exp(-0.3 * layer_idx)
    lamv = jnp.stack([p["lam_q1"], p["lam_k1"], p["lam_q2"], p["lam_k2"]]).astype(F32)
    y = _attn_a(planes, lamv, p["subln_g"].astype(F32).reshape(1, LANES), batch=batch, seq=seq, lam_init=lam_init)
    out = _out_ln(y.reshape(batch * seq, D_MODEL), p["w_out"].astype(BF16), x2d, p["ln_g"], p["ln_b"])
    return out.reshape(batch, seq, D_MODEL)


def _layer_b(x, p):
    batch, seq, _ = x.shape
    x2d = x.reshape(batch * seq, D_MODEL)
    w_in = p["w_in"]
    group_planes = []
    gate_planes = None
    for gi, (_, dil) in enumerate(B_GROUPS):
        w = w_in[:, 3 * gi * D_MODEL:3 * (gi + 1) * D_MODEL]
        if dil == 1:
            w = jnp.concatenate([w, w_in[:, 3 * len(B_GROUPS) * D_MODEL:]], axis=1)
        planes = _project(x2d, _scale_q(w), seq=seq, dil=dil, tile=TILE)
        if dil == 1:
            gate_planes = planes
        group_planes.append(planes)
    y = _attn_b(group_planes, gate_planes, 3 * SECTION_PLANES, batch=batch, seq=seq)
    out = _out_ln(y.reshape(batch * seq, D_MODEL), p["w_out"].astype(BF16), x2d, p["ln_g"], p["ln_b"])
    return out.reshape(batch, seq, D_MODEL)


def _trunk(x, layers):
    for i, p in enumerate(layers):
        x = _layer_a(x, p, i) if i % 2 == 0 else _layer_b(x, p)
    return x


def kernel(x_prompt, x_sample, w_in_0, lam_q1_0, lam_k1_0, lam_q2_0, lam_k2_0, subln_g_0, w_out_0, ln_g_0, ln_b_0, w_in_1, w_out_1, ln_g_1, ln_b_1, w_in_2, lam_q1_2, lam_k1_2, lam_q2_2, lam_k2_2, subln_g_2, w_out_2, ln_g_2, ln_b_2, w_in_3, w_out_3, ln_g_3, ln_b_3):
    layers = [
        dict(w_in=w_in_0, lam_q1=lam_q1_0, lam_k1=lam_k1_0, lam_q2=lam_q2_0, lam_k2=lam_k2_0,
             subln_g=subln_g_0, w_out=w_out_0, ln_g=ln_g_0, ln_b=ln_b_0),
        dict(w_in=w_in_1, w_out=w_out_1, ln_g=ln_g_1, ln_b=ln_b_1),
        dict(w_in=w_in_2, lam_q1=lam_q1_2, lam_k1=lam_k1_2, lam_q2=lam_q2_2, lam_k2=lam_k2_2,
             subln_g=subln_g_2, w_out=w_out_2, ln_g=ln_g_2, ln_b=ln_b_2),
        dict(w_in=w_in_3, w_out=w_out_3, ln_g=ln_g_3, ln_b=ln_b_3),
    ]
    return (_trunk(x_prompt, layers), _trunk(x_sample, layers))
```
